```python
import jax, jax.numpy as jnp
from jax import lax
import numpy as np

D_MODEL = 1024
BATCH = 32
SEQ = 2048
DEPTH = 2

GRID_W = 64
CTX_LEN = 256
CHUNK = 128
H_RET = 8
DK_RET = 128
DV_RET = 128
D_RET = H_RET * DV_RET
H_ML = 8
DH_ML = 128
D_ML = H_ML * DH_ML
CONV_W = 5
ROPE_BASE = 10000.0
EPS = 1e-6
NEG = -1e30
SPLITS = (H_RET * DK_RET, H_RET * DK_RET, D_RET, D_RET, D_ML, D_ML, D_ML, D_ML, D_ML, 4 * H_ML, D_MODEL, D_MODEL)
D_IN = sum(SPLITS)
GATE_OFF = 2 * H_RET * DK_RET + 2 * D_RET + 5 * D_ML

kernel_name = 'hybrid_retention_mlstm_dit'

F32 = jnp.float32


def _rmsnorm(x, w):
    x32 = x.astype(F32)
    y = x32 * lax.rsqrt(jnp.mean(x32 * x32, axis=-1, keepdims=True) + EPS)
    return (y * w.astype(F32)).astype(x.dtype)


def _head_norm(h, w):
    b, nh, l, d = h.shape
    h = h.astype(F32)
    hc = h - jnp.mean(h, axis=-1, keepdims=True)
    var = jnp.mean(hc * hc, axis=-1, keepdims=True)
    y = (hc * lax.rsqrt(var + EPS)).transpose(0, 2, 1, 3).reshape(b, l, nh * d)
    return y * w.astype(F32)


def _to_heads(t, nh):
    b, l, _ = t.shape
    return t.reshape(b, l, nh, -1).transpose(0, 2, 1, 3)


def _chunk(t):
    b, nh, l = t.shape[:3]
    return t.reshape(b, nh, l // CHUNK, CHUNK, *t.shape[3:])


def _flip(t):
    return jnp.flip(t, axis=2)


def _ident(t):
    return t


def _axial_rope_tables(l, d):
    rows_n = l // GRID_W
    rows = jnp.repeat(jnp.arange(rows_n, dtype=F32), GRID_W)
    cols = jnp.tile(jnp.arange(GRID_W, dtype=F32), rows_n)
    nf = d // 4
    freqs = ROPE_BASE ** (-jnp.arange(nf, dtype=F32) / nf)
    ang = jnp.concatenate([rows[:, None] * freqs, cols[:, None] * freqs], axis=-1)
    return jnp.cos(ang), jnp.sin(ang)


def _rope(t, cos, sin):
    half = t.shape[-1] // 2
    t1, t2 = t[..., :half], t[..., half:]
    return jnp.concatenate([t1 * cos - t2 * sin, t1 * sin + t2 * cos], axis=-1)


def _dwconv(t, w, b):
    ch = t.shape[-1]
    y = lax.conv_general_dilated(t, w[:, None, :], window_strides=(1,),
                                 padding=[(CONV_W // 2, CONV_W // 2)],
                                 dimension_numbers=('NWC', 'WIO', 'NWC'),
                                 feature_group_count=ch)
    return y + b


def _project(h, w_in, b_in, conv_w, conv_b, rope):
    u = jnp.einsum('bld,de->ble', h, w_in) + b_in
    idx = np.cumsum(SPLITS)[:-1].tolist()
    rq, rk, rv, rz, mq, mk, mv, mo, mz, mg, gr, gm = jnp.split(u, idx, axis=-1)
    mq, mk = jnp.split(jax.nn.silu(_dwconv(jnp.concatenate([mq, mk], axis=-1), conv_w, conv_b)), 2, axis=-1)
    rq, rk = _to_heads(rq, H_RET), _to_heads(rk, H_RET)
    if rope is not None:
        rq, rk = _rope(rq, *rope), _rope(rk, *rope)
    g = mg.astype(F32).transpose(0, 2, 1)
    ig_f, fg_f, ig_b, fg_b = jnp.split(g, 4, axis=1)
    return dict(rq=rq * DK_RET ** -0.5, rk=rk, rv=_to_heads(rv, H_RET), rz=rz,
                mq=_to_heads(mq, H_ML) * DH_ML ** -0.5, mk=_to_heads(mk, H_ML), mv=_to_heads(mv, H_ML),
                mo=mo, mz=mz, ig=(ig_f, ig_b),
                lf=(jax.nn.log_sigmoid(fg_f), jax.nn.log_sigmoid(fg_b)), gr=gr, gm=gm)


def _ret_states(k, v, lg, s0):
    kc, vc = _chunk(k.astype(F32)), _chunk(v.astype(F32))
    pos = jnp.arange(CHUNK, dtype=F32)
    lg = lg.astype(F32)
    zeta = jnp.exp(lg[:, None] * (CHUNK - 1.0 - pos))
    kv = jnp.einsum('bhncd,hc,bhnce->bhnde', kc, zeta, vc)
    decay = jnp.exp(lg * CHUNK)[:, None, None]

    def step(s, kv_n):
        return decay * s + kv_n, s

    s_fin, s_prev = lax.scan(step, s0, jnp.moveaxis(kv, 2, 0))
    return jnp.moveaxis(s_prev, 0, 2), s_fin


def _ret_outputs(q, k, v, lg, s_prev):
    b, nh, l, _ = q.shape
    qc, kc, vc = _chunk(q.astype(F32)), _chunk(k.astype(F32)), _chunk(v.astype(F32))
    pos = jnp.arange(CHUNK, dtype=F32)
    diff = pos[:, None] - pos[None, :]
    lg = lg.astype(F32)
    dmat = jnp.where(diff >= 0, jnp.exp(lg[:, None, None] * jnp.maximum(diff, 0.0)), 0.0)
    scores = jnp.einsum('bhnqd,bhnkd->bhnqk', qc, kc) * dmat[:, None]
    inner = jnp.einsum('bhnqk,bhnke->bhnqe', scores, vc)
    xi = jnp.exp(lg[:, None] * (pos + 1.0))
    cross = jnp.einsum('bhnqd,bhnde->bhnqe', qc, s_prev) * xi[:, None, :, None]
    return (inner + cross).reshape(b, nh, l, -1)


def _retention(pc, px, log_gamma, need_ctx):
    b = px['rk'].shape[0]
    s0 = jnp.zeros((b, H_RET, DK_RET, DV_RET), F32)
    out_x, out_c = [], []
    for d in range(2):
        fl = _flip if d else _ident
        lg = log_gamma[d]
        kc, vc = fl(pc['rk']), fl(pc['rv'])
        kx, vx = fl(px['rk']), fl(px['rv'])
        sp_c, sf_c = _ret_states(kc, vc, lg, s0)
        sp_x, _ = _ret_states(kx, vx, lg, sf_c)
        out_x.append(fl(_ret_outputs(fl(px['rq']), kx, vx, lg, sp_x)))
        if need_ctx:
            out_c.append(fl(_ret_outputs(fl(pc['rq']), kc, vc, lg, sp_c)))
    return out_x[0] + out_x[1], (out_c[0] + out_c[1] if need_ctx else None)


def _ml_states(k, v, ig, lf, st0):
    kc, vc = _chunk(k.astype(F32)), _chunk(v.astype(F32))
    igc, lfc = _chunk(ig), _chunk(lf)
    bcum = jnp.cumsum(lfc, axis=-1)
    g = bcum[..., -1]
    w = g[..., None] - bcum + igc
    a = jnp.max(w, axis=-1)
    ew = jnp.exp(w - a[..., None])
    kv = jnp.einsum('bhnc,bhncd,bhnce->bhnde', ew, kc, vc)
    ks = jnp.einsum('bhnc,bhncd->bhnd', ew, kc)

    def step(carry, inp):
        cm, nv, m = carry
        kv_n, ks_n, g_n, a_n = inp
        m_new = jnp.maximum(g_n + m, a_n)
        dec = jnp.exp(g_n + m - m_new)
        sc = jnp.exp(a_n - m_new)
        new = (dec[..., None, None] * cm + sc[..., None, None] * kv_n,
               dec[..., None] * nv + sc[..., None] * ks_n, m_new)
        return new, carry

    xs = tuple(jnp.moveaxis(t, 2, 0) for t in (kv, ks, g, a))
    st_fin, st_prev = lax.scan(step, st0, xs)
    return tuple(jnp.moveaxis(t, 0, 2) for t in st_prev), st_fin


def _ml_outputs(q, k, v, ig, lf, st_prev):
    b, nh, l, _ = q.shape
    qc, kc, vc = _chunk(q.astype(F32)), _chunk(k.astype(F32)), _chunk(v.astype(F32))
    igc, lfc = _chunk(ig), _chunk(lf)
    cp, npv, mp = st_prev
    bcum = jnp.cumsum(lfc, axis=-1)
    tri = jnp.tril(jnp.ones((CHUNK, CHUNK), dtype=bool))
    logd = jnp.where(tri, bcum[..., :, None] - bcum[..., None, :] + igc[..., None, :], NEG)
    m_inter = bcum + mp[..., None]
    m = jnp.maximum(jnp.max(logd, axis=-1), m_inter)
    s = jnp.einsum('bhnqd,bhnkd->bhnqk', qc, kc) * jnp.exp(logd - m[..., None])
    sc = jnp.exp(m_inter - m)
    num = jnp.einsum('bhnqk,bhnke->bhnqe', s, vc) + sc[..., None] * jnp.einsum('bhnqd,bhnde->bhnqe', qc, cp)
    den = jnp.sum(s, axis=-1) + sc * jnp.einsum('bhnqd,bhnd->bhnq', qc, npv)
    den = jnp.maximum(jnp.abs(den), jnp.exp(-m))
    return (num / den[..., None]).reshape(b, nh, l, -1)


def _mlstm(pc, px, need_ctx):
    b = px['mk'].shape[0]
    st0 = (jnp.zeros((b, H_ML, DH_ML, DH_ML), F32), jnp.zeros((b, H_ML, DH_ML), F32), jnp.zeros((b, H_ML), F32))
    out_x, out_c = [], []
    for d in range(2):
        fl = _flip if d else _ident
        args_c = (fl(pc['mk']), fl(pc['mv']), fl(pc['ig'][d]), fl(pc['lf'][d]))
        args_x = (fl(px['mk']), fl(px['mv']), fl(px['ig'][d]), fl(px['lf'][d]))
        sp_c, sf_c = _ml_states(*args_c, st0)
        sp_x, _ = _ml_states(*args_x, sf_c)
        out_x.append(fl(_ml_outputs(fl(px['mq']), *args_x, sp_x)))
        if need_ctx:
            out_c.append(fl(_ml_outputs(fl(pc['mq']), *args_c, sp_c)))
    return out_x[0] + out_x[1], (out_c[0] + out_c[1] if need_ctx else None)


def _merge(p, ret_h, ml_h, ret_norm_w, ml_norm_w, w_ret_o, w_ml_o, w_out):
    dt = w_out.dtype
    yr = _head_norm(ret_h, ret_norm_w) * jax.nn.silu(p['rz'].astype(F32))
    o = _to_heads(jax.nn.sigmoid(p['mo'].astype(F32)), H_ML)
    ym = _head_norm(o * ml_h, ml_norm_w) * jax.nn.silu(p['mz'].astype(F32))
    br = jnp.einsum('ble,ed->bld', yr.astype(dt), w_ret_o)
    bm = jnp.einsum('ble,ed->bld', ym.astype(dt), w_ml_o)
    y = jax.nn.sigmoid(p['gr']) * br + jax.nn.sigmoid(p['gm']) * bm
    return jnp.einsum('bld,de->ble', y, w_out)


def _layer(x, ctx, mod_x, mod_c, norm_w, w_in, b_in, conv_w, conv_b, log_gamma,
           ret_norm_w, ml_norm_w, w_ret_o, w_ml_o, w_out, rope, need_ctx):
    sh_x, sc_x, g_x = jnp.split(mod_x, 3, axis=-1)
    sh_c, sc_c, g_c = jnp.split(mod_c, 3, axis=-1)
    hx = _rmsnorm(x, norm_w) * (1.0 + sc_x[:, None]) + sh_x[:, None]
    hc = _rmsnorm(ctx, norm_w) * (1.0 + sc_c) + sh_c
    px = _project(hx, w_in, b_in, conv_w, conv_b, rope)
    pc = _project(hc, w_in, b_in, conv_w, conv_b, None)
    rx, rc = _retention(pc, px, log_gamma, need_ctx)
    mx, mc = _mlstm(pc, px, need_ctx)
    x = x + (g_x[:, None] * _merge(px, rx, mx, ret_norm_w, ml_norm_w, w_ret_o, w_ml_o, w_out)).astype(x.dtype)
    if need_ctx:
        ctx = ctx + (g_c * _merge(pc, rc, mc, ret_norm_w, ml_norm_w, w_ret_o, w_ml_o, w_out)).astype(ctx.dtype)
    return x, ctx


def setup_inputs(seed: int = 0) -> dict:
    key = jax.random.key(seed)
    ks = jax.random.split(key, 18)

    def nrm(k, shape, scale):
        return scale * jax.random.normal(k, shape, F32)

    x = nrm(ks[0], (BATCH, SEQ, D_MODEL), 1.0)
    c = nrm(ks[1], (BATCH, D_MODEL), 1.0)
    ctx = nrm(ks[2], (BATCH, CTX_LEN, D_MODEL), 1.0)
    c_ctx = nrm(ks[3], (D_MODEL,), 1.0)
    norm_w = 1.0 + nrm(ks[4], (DEPTH, D_MODEL), 0.02)
    w_ada = nrm(ks[5], (DEPTH, D_MODEL, 3 * D_MODEL), 0.5 * D_MODEL ** -0.5)
    b_ada = nrm(ks[6], (DEPTH, 3 * D_MODEL), 0.01)
    w_in = nrm(ks[7], (DEPTH, D_MODEL, D_IN), D_MODEL ** -0.5)
    f_bias = jnp.linspace(3.0, 6.0, H_ML, dtype=F32)
    z = jnp.zeros((H_ML,), F32)
    gate_bias = jnp.concatenate([z, f_bias, z, f_bias])
    b_in = nrm(ks[8], (DEPTH, D_IN), 0.01).at[:, GATE_OFF:GATE_OFF + 4 * H_ML].add(gate_bias)
    conv_w = nrm(ks[9], (DEPTH, CONV_W, 2 * D_ML), CONV_W ** -0.5)
    conv_b = nrm(ks[10], (DEPTH, 2 * D_ML), 0.01)
    base = jnp.log1p(-(2.0 ** (-5.0 - jnp.arange(H_RET, dtype=F32))))
    ret_log_gamma = base * (1.0 + nrm(ks[11], (DEPTH, 2, H_RET), 0.05))
    ret_norm_w = 1.0 + nrm(ks[12], (DEPTH, D_RET), 0.02)
    ml_norm_w = 1.0 + nrm(ks[13], (DEPTH, D_ML), 0.02)
    w_ret_o = nrm(ks[14], (DEPTH, D_RET, D_MODEL), D_RET ** -0.5)
    w_ml_o = nrm(ks[15], (DEPTH, D_ML, D_MODEL), D_ML ** -0.5)
    w_out = nrm(ks[16], (DEPTH, D_MODEL, D_MODEL), D_MODEL ** -0.5)
    final_norm_w = 1.0 + nrm(ks[17], (D_MODEL,), 0.02)
    return {'x': x, 'c': c, 'ctx': ctx, 'c_ctx': c_ctx, 'norm_w': norm_w, 'w_ada': w_ada, 'b_ada': b_ada,
            'w_in': w_in, 'b_in': b_in, 'conv_w': conv_w, 'conv_b': conv_b, 'ret_log_gamma': ret_log_gamma,
            'ret_norm_w': ret_norm_w, 'ml_norm_w': ml_norm_w, 'w_ret_o': w_ret_o, 'w_ml_o': w_ml_o,
            'w_out': w_out, 'final_norm_w': final_norm_w}


def reference(x, c, ctx, c_ctx, norm_w, w_ada, b_ada, w_in, b_in, conv_w, conv_b, ret_log_gamma,
              ret_norm_w, ml_norm_w, w_ret_o, w_ml_o, w_out, final_norm_w):
    rope = _axial_rope_tables(x.shape[1], DK_RET)
    for l in range(DEPTH):
        need_ctx = l < DEPTH - 1
        mod_x = jax.nn.silu(c) @ w_ada[l] + b_ada[l]
        mod_c = jax.nn.silu(c_ctx) @ w_ada[l] + b_ada[l]
        x, ctx = _layer(x, ctx, mod_x, mod_c, norm_w[l], w_in[l], b_in[l], conv_w[l], conv_b[l],
                        ret_log_gamma[l], ret_norm_w[l], ml_norm_w[l], w_ret_o[l], w_ml_o[l], w_out[l],
                        rope, need_ctx)
    return _rmsnorm(x, final_norm_w)
```

```python
import functools

import jax
import jax.numpy as jnp
from jax import lax
from jax.experimental import pallas as pl
from jax.experimental.pallas import tpu as pltpu

F32 = jnp.float32
BF16 = jnp.bfloat16

N_HEADS = 8
D_HEAD = 128
CHUNK = 128
CONV_W = 5
GRID_W = 64
ROPE_BASE = 10000.0
EPS = 1e-6
NEG = -1e30
N_GATE_ROWS = 8
ONES_ROWS = 16
AUG = D_HEAD + ONES_ROWS

VMEM_LIMIT_BYTES = 56 * 1024 * 1024

NT_DIMS = (((1,), (1,)), ((), ()))
TN_DIMS = (((0,), (0,)), ((), ()))

(G_ROWK_F, G_ROWK_B, G_CM_F, G_CM_B, G_TOT_F, G_TOT_B, G_RMAX_F, G_RMAX_B, G_BCUM_F, G_BCUM_B,
 G_COLQ_F, G_COLQ_B, G_SC_F, G_SC_B, G_EMM_F, G_EMM_B, G_EW_F, G_EW_B, G_DEC_F, G_DEC_B) = range(20)
G_ROWS = 24


def _silu(v):
    return v * jax.nn.sigmoid(v)


def _params(*sem):
    return pltpu.CompilerParams(dimension_semantics=sem, vmem_limit_bytes=VMEM_LIMIT_BYTES)


def _adaln_kernel(c_ref, w_ref, b_ref, o_ref):
    s = _silu(c_ref[...])
    o_ref[0] = jnp.dot(s, w_ref[0], preferred_element_type=F32,
                       precision=lax.Precision.HIGHEST) + b_ref[0]


def _adaln(cc, w_ada, b_ada):
    depth, d, d3 = w_ada.shape
    rows = cc.shape[0]
    tn = 512
    return pl.pallas_call(
        _adaln_kernel,
        grid=(depth, d3 // tn),
        in_specs=[pl.BlockSpec((rows, d), lambda l, j: (0, 0)),
                  pl.BlockSpec((1, d, tn), lambda l, j: (l, 0, j)),
                  pl.BlockSpec((1, 1, tn), lambda l, j: (l, 0, j))],
        out_specs=pl.BlockSpec((1, rows, tn), lambda l, j: (l, 0, j)),
        out_shape=jax.ShapeDtypeStruct((depth, rows, d3), F32),
        compiler_params=_params("parallel", "parallel"),
        name="adaln",
    )(cc, w_ada, b_ada.reshape(depth, 1, d3))


def _norm_mod_to_scratch(x_ref, nw_ref, mx_ref, mc_ref, h_scr, n_ctx):
    rows = 256
    t = x_ref.shape[1]

    def body(i, carry):
        r0 = pl.multiple_of(i * rows, rows)
        xv = x_ref[0, pl.ds(r0, rows), :]
        ms = jnp.mean(xv * xv, axis=-1, keepdims=True)
        y = xv * lax.rsqrt(ms + EPS) * nw_ref[...]
        is_ctx = i < n_ctx // rows
        sh = jnp.where(is_ctx, mc_ref[0:1, :], mx_ref[0, 0:1, :])
        sc = jnp.where(is_ctx, mc_ref[1:2, :], mx_ref[0, 1:2, :])
        h_scr[pl.ds(r0, rows), :] = (y * (1.0 + sc) + sh).astype(BF16)
        return carry

    lax.fori_loop(0, t // rows, body, 0)


def _proj_fm_kernel(x_ref, nw_ref, mx_ref, mc_ref, wt_ref, bt_ref, wg_ref, bg_ref,
                    ut_ref, g_ref, h_scr, *, n_ctx):
    @pl.when(pl.program_id(1) == 0)
    def _():
        _norm_mod_to_scratch(x_ref, nw_ref, mx_ref, mc_ref, h_scr, n_ctx)
        g = lax.dot_general(wg_ref[...], h_scr[...], NT_DIMS, preferred_element_type=F32)
        g_ref[0] = g + bg_ref[...]

    acc = lax.dot_general(wt_ref[...], h_scr[...], NT_DIMS, preferred_element_type=F32)
    ut_ref[0] = (acc + bt_ref[...]).astype(BF16)


def _proj_tm_kernel(x_ref, nw_ref, mx_ref, mc_ref, wk_ref, bk_ref, uk_ref, h_scr, *, n_ctx):
    @pl.when(pl.program_id(1) == 0)
    def _():
        _norm_mod_to_scratch(x_ref, nw_ref, mx_ref, mc_ref, h_scr, n_ctx)

    acc = jnp.dot(h_scr[...], wk_ref[...], preferred_element_type=F32)
    uk_ref[0] = (acc + bk_ref[...]).astype(BF16)


def _proj(xs, nw, mx, mc, wt, bt, wg, bg, wk, bk, n_ctx):
    b, t, d = xs.shape
    n_fm = wt.shape[0]
    n_tm = wk.shape[1]
    tn = 512
    common = [pl.BlockSpec((1, t, d), lambda i, j: (i, 0, 0)),
              pl.BlockSpec((1, d), lambda i, j: (0, 0)),
              pl.BlockSpec((1, 3, d), lambda i, j: (i, 0, 0)),
              pl.BlockSpec((3, d), lambda i, j: (0, 0))]
    ut, gt = pl.pallas_call(
        functools.partial(_proj_fm_kernel, n_ctx=n_ctx),
        grid=(b, n_fm // tn),
        in_specs=common + [pl.BlockSpec((tn, d), lambda i, j: (j, 0)),
                           pl.BlockSpec((tn, 1), lambda i, j: (j, 0)),
                           pl.BlockSpec(wg.shape, lambda i, j: (0, 0)),
                           pl.BlockSpec(bg.shape, lambda i, j: (0, 0))],
        out_specs=[pl.BlockSpec((1, tn, t), lambda i, j: (i, j, 0)),
                   pl.BlockSpec((1, wg.shape[0], t), lambda i, j: (i, 0, 0))],
        out_shape=[jax.ShapeDtypeStruct((b, n_fm, t), BF16),
                   jax.ShapeDtypeStruct((b, wg.shape[0], t), F32)],
        scratch_shapes=[pltpu.VMEM((t, d), BF16)],
        compiler_params=_params("parallel", "arbitrary"),
        name="proj_fm",
    )(xs, nw, mx, mc, wt, bt, wg, bg)
    uk = pl.pallas_call(
        functools.partial(_proj_tm_kernel, n_ctx=n_ctx),
        grid=(b, n_tm // tn),
        in_specs=common + [pl.BlockSpec((d, tn), lambda i, j: (0, j)),
                           pl.BlockSpec((1, tn), lambda i, j: (0, j))],
        out_specs=pl.BlockSpec((1, t, tn), lambda i, j: (i, 0, j)),
        out_shape=jax.ShapeDtypeStruct((b, t, n_tm), BF16),
        scratch_shapes=[pltpu.VMEM((t, d), BF16)],
        compiler_params=_params("parallel", "arbitrary"),
        name="proj_tm",
    )(xs, nw, mx, mc, wk, bk)
    return ut, gt, uk


def _chunk_start(c):
    return pl.multiple_of(c * CHUNK, CHUNK)


def _loop(lo, hi, body, reverse=False):
    n = hi - lo
    if n <= 0:
        return

    def wrapped(i, carry):
        body(hi - 1 - i if reverse else lo + i)
        return carry

    lax.fori_loop(0, n, wrapped, 0)


def _scan(lo, hi, body, init, reverse=False):
    n = hi - lo
    if n <= 0:
        return init

    def wrapped(i, carry):
        return body(hi - 1 - i if reverse else lo + i, carry)

    return lax.fori_loop(0, n, wrapped, init)


def _head_norm_gate(o, nw_col, z):
    mu = jnp.mean(o, axis=0, keepdims=True)
    oc = o - mu
    var = jnp.mean(oc * oc, axis=0, keepdims=True)
    return oc * lax.rsqrt(var + EPS) * nw_col * _silu(z)


def _ret_kernel(lg_ref, qt_ref, k_ref, vt_ref, zt_ref, cos_t_ref, sin_t_ref, cos_k_ref, sin_k_ref,
                nw_ref, y_ref, qs, ks, kv, st, *, n_ctx_chunks, n_chunks, out_ctx):
    c_ = CHUNK
    h = pl.program_id(1)
    lg_f = lg_ref[0, h]
    lg_b = lg_ref[1, h]
    lane = lax.broadcasted_iota(jnp.int32, (1, c_), 1).astype(F32)
    zeta_f = jnp.exp(lg_f * (c_ - 1.0 - lane))
    zeta_b = jnp.exp(lg_b * lane)
    xi_f = jnp.exp(lg_f * (lane + 1.0))
    xi_b = jnp.exp(lg_b * (c_ - lane))
    dec_f = jnp.exp(lg_f * jnp.full((1, c_), float(c_), F32))
    dec_b = jnp.exp(lg_b * jnp.full((1, c_), float(c_), F32))
    ki = lax.broadcasted_iota(jnp.int32, (c_, c_), 0)
    qi = lax.broadcasted_iota(jnp.int32, (c_, c_), 1)
    diff = (qi - ki).astype(F32)
    dsum = (jnp.where(diff >= 0, jnp.exp(lg_f * jnp.maximum(diff, 0.0)), 0.0)
            + jnp.where(diff <= 0, jnp.exp(lg_b * jnp.maximum(-diff, 0.0)), 0.0))
    scale = D_HEAD ** -0.5

    def prep_ctx(c):
        t0 = _chunk_start(c)
        qs[:, pl.ds(t0, c_)] = (qt_ref[0, :, pl.ds(t0, c_)].astype(F32) * scale).astype(BF16)
        ks[pl.ds(t0, c_), :] = k_ref[0, pl.ds(t0, c_), :]

    def prep_lat(c):
        t0 = _chunk_start(c)
        l0 = _chunk_start(c - n_ctx_chunks)
        q = qt_ref[0, :, pl.ds(t0, c_)].astype(F32)
        k = k_ref[0, pl.ds(t0, c_), :].astype(F32)
        qr = q * cos_t_ref[:, pl.ds(l0, c_)] + pltpu.roll(q, D_HEAD // 2, 0) * sin_t_ref[:, pl.ds(l0, c_)]
        kr = k * cos_k_ref[pl.ds(l0, c_), :] + pltpu.roll(k, D_HEAD // 2, 1) * sin_k_ref[pl.ds(l0, c_), :]
        qs[:, pl.ds(t0, c_)] = (qr * scale).astype(BF16)
        ks[pl.ds(t0, c_), :] = kr.astype(BF16)

    _loop(0, n_ctx_chunks, prep_ctx)
    _loop(n_ctx_chunks, n_chunks, prep_lat)

    def outer_products(c):
        t0 = _chunk_start(c)
        vt = vt_ref[0, :, pl.ds(t0, c_)].astype(F32)
        lhs = jnp.concatenate([(vt * zeta_f).astype(BF16), (vt * zeta_b).astype(BF16)], axis=0)
        kv[c] = jnp.dot(lhs, ks[pl.ds(t0, c_), :], preferred_element_type=F32)

    _loop(0, n_chunks, outer_products)

    def scan_f(c, s):
        st[c, :, 0:c_] = s.astype(BF16)
        return dec_f * s + kv[c, 0:c_, :]

    def scan_b(c, s):
        st[c, :, c_:2 * c_] = s.astype(BF16)
        return dec_b * s + kv[c, c_:2 * c_, :]

    zero = jnp.zeros((c_, c_), F32)
    _scan(0, n_chunks, scan_f, zero)
    sb = _scan(0, n_ctx_chunks, scan_b, zero, reverse=True)
    _scan(n_ctx_chunks, n_chunks, scan_b, sb, reverse=True)

    nw_col = nw_ref[...]

    def outputs(c):
        t0 = _chunk_start(c)
        qt = qs[:, pl.ds(t0, c_)]
        at = jnp.dot(ks[pl.ds(t0, c_), :], qt, preferred_element_type=F32)
        pt = (at * dsum).astype(BF16)
        qf = qt.astype(F32)
        rhs = jnp.concatenate([(qf * xi_f).astype(BF16), (qf * xi_b).astype(BF16)], axis=0)
        o = (jnp.dot(vt_ref[0, :, pl.ds(t0, c_)], pt, preferred_element_type=F32)
             + jnp.dot(st[c], rhs, preferred_element_type=F32))
        z = zt_ref[0, :, pl.ds(t0, c_)].astype(F32)
        y_ref[0, :, pl.ds(t0, c_)] = _head_norm_gate(o, nw_col, z).astype(BF16)

    if out_ctx:
        _loop(0, n_chunks, outputs)
    else:
        y_ref[0, :, 0:n_ctx_chunks * c_] = jnp.zeros((D_HEAD, n_ctx_chunks * c_), BF16)
        _loop(n_ctx_chunks, n_chunks, outputs)


def _ret_mix(lg, ut, uk, tables, nw, n_ctx, out_ctx):
    b, _, t = ut.shape
    n_chunks = t // CHUNK
    hd = D_HEAD
    cos_t, sin_t, cos_k, sin_k = tables
    n_lat = cos_k.shape[0]

    def fm(group):
        return pl.BlockSpec((1, hd, t), lambda i, h: (i, group * N_HEADS + h, 0))

    kern = functools.partial(_ret_kernel, n_ctx_chunks=n_ctx // CHUNK, n_chunks=n_chunks, out_ctx=out_ctx)
    return pl.pallas_call(
        kern,
        grid=(b, N_HEADS),
        in_specs=[pl.BlockSpec(memory_space=pltpu.SMEM),
                  fm(0),
                  pl.BlockSpec((1, t, hd), lambda i, h: (i, 0, h)),
                  fm(1), fm(2),
                  pl.BlockSpec((hd, n_lat), lambda i, h: (0, 0)),
                  pl.BlockSpec((hd, n_lat), lambda i, h: (0, 0)),
                  pl.BlockSpec((n_lat, hd), lambda i, h: (0, 0)),
                  pl.BlockSpec((n_lat, hd), lambda i, h: (0, 0)),
                  pl.BlockSpec((hd, 1), lambda i, h: (h, 0))],
        out_specs=pl.BlockSpec((1, hd, t), lambda i, h: (i, h, 0)),
        out_shape=jax.ShapeDtypeStruct((b, N_HEADS * hd, t), BF16),
        scratch_shapes=[pltpu.VMEM((hd, t), BF16),
                        pltpu.VMEM((t, hd), BF16),
                        pltpu.VMEM((n_chunks, 2 * hd, hd), F32),
                        pltpu.VMEM((n_chunks, hd, 2 * hd), BF16)],
        compiler_params=_params("parallel", "arbitrary"),
        name="ret_mix",
    )(lg, ut, uk, ut, ut, cos_t, sin_t, cos_k, sin_k, nw)


def _shifted(prev, cur, nxt, s, axis):
    n = cur.shape[axis]
    idx = lax.broadcasted_iota(jnp.int32, cur.shape, axis)
    if s > 0:
        return jnp.where(idx < n - s, pltpu.roll(cur, n - s, axis), pltpu.roll(nxt, n - s, axis))
    return jnp.where(idx >= -s, pltpu.roll(cur, -s, axis), pltpu.roll(prev, -s, axis))


def _seg_scan(v, pos, op, fill, reverse):
    t = v.shape[1]
    step = 1
    while step < CHUNK:
        if reverse:
            v = op(v, jnp.where(pos < CHUNK - step, pltpu.roll(v, t - step, 1), fill))
        else:
            v = op(v, jnp.where(pos >= step, pltpu.roll(v, step, 1), fill))
        step *= 2
    return v


def _ml_gate_rows(g_ref, gs, n_ctx_chunks, n_chunks):
    c_ = CHUNK
    g = g_ref[0]
    t = g.shape[1]
    pos = lax.broadcasted_iota(jnp.int32, g.shape, 1) & (c_ - 1)
    sub = lax.broadcasted_iota(jnp.int32, g.shape, 0)
    lf = jax.nn.log_sigmoid(g)
    psum = _seg_scan(lf, pos, jnp.add, 0.0, False)
    ssum = _seg_scan(lf, pos, jnp.add, 0.0, True)
    tot = psum + ssum - lf
    r = jnp.where(sub == 0, g - pltpu.roll(psum, N_GATE_ROWS - 1, 0), g - pltpu.roll(ssum, N_GATE_ROWS - 1, 0))
    pmax = _seg_scan(r, pos, jnp.maximum, NEG, False)
    smax = _seg_scan(r, pos, jnp.maximum, NEG, True)
    rmax = jnp.maximum(pmax, smax)
    gs[G_ROWK_F:G_ROWK_F + 1, :] = r[0:1]
    gs[G_ROWK_B:G_ROWK_B + 1, :] = r[2:3]
    gs[G_CM_F:G_CM_F + 1, :] = pmax[0:1]
    gs[G_CM_B:G_CM_B + 1, :] = smax[2:3]
    gs[G_TOT_F:G_TOT_F + 1, :] = tot[1:2]
    gs[G_TOT_B:G_TOT_B + 1, :] = tot[3:4]
    gs[G_RMAX_F:G_RMAX_F + 1, :] = rmax[0:1]
    gs[G_RMAX_B:G_RMAX_B + 1, :] = rmax[2:3]
    gs[G_BCUM_F:G_BCUM_F + 1, :] = psum[1:2]
    gs[G_BCUM_B:G_BCUM_B + 1, :] = ssum[3:4]

    def make_step(d):
        def step(c, m_prev):
            sl = pl.ds(_chunk_start(c), c_)
            gt = gs[G_TOT_F + d:G_TOT_F + d + 1, sl]
            rr = gs[G_ROWK_F + d:G_ROWK_F + d + 1, sl]
            m_new = jnp.maximum(gt + m_prev, gt + gs[G_RMAX_F + d:G_RMAX_F + d + 1, sl])
            mx = jnp.maximum(gs[G_CM_F + d:G_CM_F + d + 1, sl], m_prev)
            gs[G_COLQ_F + d:G_COLQ_F + d + 1, sl] = -mx
            gs[G_SC_F + d:G_SC_F + d + 1, sl] = jnp.exp(m_prev - mx)
            gs[G_EMM_F + d:G_EMM_F + d + 1, sl] = jnp.exp(-(gs[G_BCUM_F + d:G_BCUM_F + d + 1, sl] + mx))
            gs[G_EW_F + d:G_EW_F + d + 1, sl] = jnp.exp(gt + rr - m_new)
            gs[G_DEC_F + d:G_DEC_F + d + 1, sl] = jnp.exp(gt + m_prev - m_new)
            return m_new
        return step

    m0 = jnp.zeros((1, c_), F32)
    _scan(0, n_chunks, make_step(0), m0)
    mb = _scan(0, n_ctx_chunks, make_step(1), m0, reverse=True)
    _scan(n_ctx_chunks, n_chunks, make_step(1), mb, reverse=True)


def _ml_kernel(qt_ref, k_ref, vt_ref, ot_ref, zt_ref, g_ref, cwq_ref, cbq_ref, cwk_ref, cbk_ref, nw_ref,
               y_ref, qs, ks, kv, st, gs, *, n_ctx_chunks, n_chunks, out_ctx):
    c_ = CHUNK
    scale = D_HEAD ** -0.5

    _ml_gate_rows(g_ref, gs, n_ctx_chunks, n_chunks)

    def conv_segment(lo, hi):
        def body(c):
            t0 = _chunk_start(c)
            tp = _chunk_start(jnp.maximum(c - 1, lo))
            tn = _chunk_start(jnp.minimum(c + 1, hi - 1))
            has_p = c > lo
            has_n = c < hi - 1
            q_c = qt_ref[0, :, pl.ds(t0, c_)].astype(F32)
            q_p = jnp.where(has_p, qt_ref[0, :, pl.ds(tp, c_)].astype(F32), 0.0)
            q_n = jnp.where(has_n, qt_ref[0, :, pl.ds(tn, c_)].astype(F32), 0.0)
            k_c = k_ref[0, pl.ds(t0, c_), :].astype(F32)
            k_p = jnp.where(has_p, k_ref[0, pl.ds(tp, c_), :].astype(F32), 0.0)
            k_n = jnp.where(has_n, k_ref[0, pl.ds(tn, c_), :].astype(F32), 0.0)
            half = CONV_W // 2
            qa = q_c * cwq_ref[half] + cbq_ref[...]
            ka = k_c * cwk_ref[half:half + 1, :] + cbk_ref[...]
            for j in range(CONV_W):
                s = j - half
                if s == 0:
                    continue
                qa = qa + _shifted(q_p, q_c, q_n, s, 1) * cwq_ref[j]
                ka = ka + _shifted(k_p, k_c, k_n, s, 0) * cwk_ref[j:j + 1, :]
            qs[:, pl.ds(t0, c_)] = (_silu(qa) * scale).astype(BF16)
            ks[pl.ds(t0, c_), :] = _silu(ka).astype(BF16)
        _loop(lo, hi, body)

    conv_segment(0, n_ctx_chunks)
    conv_segment(n_ctx_chunks, n_chunks)

    ones = jnp.ones((ONES_ROWS, c_), F32)

    def outer_products(c):
        sl = pl.ds(_chunk_start(c), c_)
        vt = vt_ref[0, :, sl].astype(F32)
        ew_f = gs[G_EW_F:G_EW_F + 1, sl]
        ew_b = gs[G_EW_B:G_EW_B + 1, sl]
        lhs = jnp.concatenate([(vt * ew_f).astype(BF16), (ones * ew_f).astype(BF16),
                               (vt * ew_b).astype(BF16), (ones * ew_b).astype(BF16)], axis=0)
        kv[c] = jnp.dot(lhs, ks[sl, :], preferred_element_type=F32)

    _loop(0, n_chunks, outer_products)

    def make_scan(d):
        def step(c, s):
            sl = pl.ds(_chunk_start(c), c_)
            st[c, d * AUG:(d + 1) * AUG, :] = s.astype(BF16)
            return gs[G_DEC_F + d:G_DEC_F + d + 1, sl] * s + kv[c, d * AUG:(d + 1) * AUG, :]
        return step

    zero = jnp.zeros((AUG, c_), F32)
    _scan(0, n_chunks, make_scan(0), zero)
    sb = _scan(0, n_ctx_chunks, make_scan(1), zero, reverse=True)
    _scan(n_ctx_chunks, n_chunks, make_scan(1), sb, reverse=True)

    nw_col = nw_ref[...]
    ki = lax.broadcasted_iota(jnp.int32, (c_, c_), 0)
    qi = lax.broadcasted_iota(jnp.int32, (c_, c_), 1)

    def outputs(c):
        sl = pl.ds(_chunk_start(c), c_)
        qt = qs[:, sl]
        qf = qt.astype(F32)
        at = jnp.dot(ks[sl, :], qt, preferred_element_type=F32)
        rowk_f = jnp.broadcast_to(gs[G_ROWK_F:G_ROWK_F + 1, sl], (c_, c_)).T
        rowk_b = jnp.broadcast_to(gs[G_ROWK_B:G_ROWK_B + 1, sl], (c_, c_)).T
        s_f = at * jnp.exp(jnp.where(ki <= qi, rowk_f + gs[G_COLQ_F:G_COLQ_F + 1, sl], NEG))
        s_b = at * jnp.exp(jnp.where(ki >= qi, rowk_b + gs[G_COLQ_B:G_COLQ_B + 1, sl], NEG))
        crs_f = jnp.dot(st[c, 0:AUG, :], (qf * gs[G_SC_F:G_SC_F + 1, sl]).astype(BF16),
                        preferred_element_type=F32)
        crs_b = jnp.dot(st[c, AUG:2 * AUG, :], (qf * gs[G_SC_B:G_SC_B + 1, sl]).astype(BF16),
                        preferred_element_type=F32)
        den_f = jnp.sum(s_f, axis=0, keepdims=True) + crs_f[D_HEAD:D_HEAD + 1, :]
        den_b = jnp.sum(s_b, axis=0, keepdims=True) + crs_b[D_HEAD:D_HEAD + 1, :]
        r_f = 1.0 / jnp.maximum(jnp.abs(den_f), gs[G_EMM_F:G_EMM_F + 1, sl])
        r_b = 1.0 / jnp.maximum(jnp.abs(den_b), gs[G_EMM_B:G_EMM_B + 1, sl])
        pt = (s_f * r_f + s_b * r_b).astype(BF16)
        o = (jnp.dot(vt_ref[0, :, sl], pt, preferred_element_type=F32)
             + crs_f[0:D_HEAD, :] * r_f + crs_b[0:D_HEAD, :] * r_b)
        o = jax.nn.sigmoid(ot_ref[0, :, sl].astype(F32)) * o
        z = zt_ref[0, :, sl].astype(F32)
        y_ref[0, :, sl] = _head_norm_gate(o, nw_col, z).astype(BF16)

    if out_ctx:
        _loop(0, n_chunks, outputs)
    else:
        y_ref[0, :, 0:n_ctx_chunks * c_] = jnp.zeros((D_HEAD, n_ctx_chunks * c_), BF16)
        _loop(n_ctx_chunks, n_chunks, outputs)


def _ml_mix(ut, uk, gt, cwq, cbq, cwk, cbk, nw, n_ctx, out_ctx):
    b, _, t = ut.shape
    n_chunks = t // CHUNK
    hd = D_HEAD

    def fm(group):
        return pl.BlockSpec((1, hd, t), lambda i, h: (i, group * N_HEADS + h, 0))

    kern = functools.partial(_ml_kernel, n_ctx_chunks=n_ctx // CHUNK, n_chunks=n_chunks, out_ctx=out_ctx)
    return pl.pallas_call(
        kern,
        grid=(b, N_HEADS),
        in_specs=[fm(3),
                  pl.BlockSpec((1, t, hd), lambda i, h: (i, 0, N_HEADS + h)),
                  fm(4), fm(5), fm(6),
                  pl.BlockSpec((1, N_GATE_ROWS, t), lambda i, h: (i, h, 0)),
                  pl.BlockSpec((CONV_W, hd, 1), lambda i, h: (0, h, 0)),
                  pl.BlockSpec((hd, 1), lambda i, h: (h, 0)),
                  pl.BlockSpec((CONV_W, hd), lambda i, h: (0, h)),
                  pl.BlockSpec((1, hd), lambda i, h: (0, h)),
                  pl.BlockSpec((hd, 1), lambda i, h: (h, 0))],
        out_specs=pl.BlockSpec((1, hd, t), lambda i, h: (i, h, 0)),
        out_shape=jax.ShapeDtypeStruct((b, N_HEADS * hd, t), BF16),
        scratch_shapes=[pltpu.VMEM((hd, t), BF16),
                        pltpu.VMEM((t, hd), BF16),
                        pltpu.VMEM((n_chunks, 2 * AUG, hd), F32),
                        pltpu.VMEM((n_chunks, 2 * AUG, hd), BF16),
                        pltpu.VMEM((G_ROWS, t), F32)],
        compiler_params=_params("parallel", "arbitrary"),
        name="ml_mix",
    )(ut, uk, ut, ut, ut, gt, cwq, cbq, cwk, cbk, nw)


def _out_kernel(yr_ref, ym_ref, gr_ref, gm_ref, wro_ref, wmo_ref, wout_ref, x_ref, gate_ref, fnw_ref,
                o_ref, *, final):
    br = jnp.dot(wro_ref[...], yr_ref[0], preferred_element_type=F32)
    bm = jnp.dot(wmo_ref[...], ym_ref[0], preferred_element_type=F32)
    y = (jax.nn.sigmoid(gr_ref[0].astype(F32)) * br + jax.nn.sigmoid(gm_ref[0].astype(F32)) * bm).astype(BF16)
    o = lax.dot_general(y, wout_ref[...], TN_DIMS, preferred_element_type=F32)
    xn = x_ref[0] + gate_ref[0, 0] * o
    if final:
        ms = jnp.mean(xn * xn, axis=-1, keepdims=True)
        xn = xn * lax.rsqrt(ms + EPS) * fnw_ref[...]
    o_ref[0] = xn


def _out(yr, ym, ut, wro_t, wmo_t, wout, xs, gates, fnw, n_ctx, final):
    b, t, d = xs.shape
    tm = 256
    skip = n_ctx // tm if final else 0
    n_tiles = t // tm - skip
    ctx_tiles = n_ctx // tm
    gr_blk = 7 * N_HEADS * D_HEAD // d
    gm_blk = 8 * N_HEADS * D_HEAD // d
    kern = functools.partial(_out_kernel, final=final)
    return pl.pallas_call(
        kern,
        grid=(b, n_tiles),
        in_specs=[pl.BlockSpec((1, d, tm), lambda i, j: (i, 0, j + skip)),
                  pl.BlockSpec((1, d, tm), lambda i, j: (i, 0, j + skip)),
                  pl.BlockSpec((1, d, tm), lambda i, j: (i, gr_blk, j + skip)),
                  pl.BlockSpec((1, d, tm), lambda i, j: (i, gm_blk, j + skip)),
                  pl.BlockSpec((d, d), lambda i, j: (0, 0)),
                  pl.BlockSpec((d, d), lambda i, j: (0, 0)),
                  pl.BlockSpec((d, d), lambda i, j: (0, 0)),
                  pl.BlockSpec((1, tm, d), lambda i, j: (i, j + skip, 0)),
                  pl.BlockSpec((1, 1, 1, d), lambda i, j: (i, jnp.where(j + skip < ctx_tiles, 0, 1), 0, 0)),
                  pl.BlockSpec((1, d), lambda i, j: (0, 0))],
        out_specs=pl.BlockSpec((1, tm, d), lambda i, j: (i, j, 0)),
        out_shape=jax.ShapeDtypeStruct((b, n_tiles * tm, d), F32),
        compiler_params=_params("parallel", "parallel"),
        name="out_final" if final else "out",
    )(yr, ym, ut, ut, wro_t, wmo_t, wout, xs, gates, fnw)


def _rope_tables(n_lat):
    rows_n = n_lat // GRID_W
    rows = jnp.repeat(jnp.arange(rows_n, dtype=F32), GRID_W)
    cols = jnp.tile(jnp.arange(GRID_W, dtype=F32), rows_n)
    nf = D_HEAD // 4
    freqs = ROPE_BASE ** (-jnp.arange(nf, dtype=F32) / nf)
    ang = jnp.concatenate([rows[:, None] * freqs, cols[:, None] * freqs], axis=-1)
    cos, sin = jnp.cos(ang), jnp.sin(ang)
    cos_k = jnp.concatenate([cos, cos], axis=-1)
    sin_k = jnp.concatenate([-sin, sin], axis=-1)
    return cos_k.T, sin_k.T, cos_k, sin_k


def _split_weights(w_in, b_in):
    hd = N_HEADS * D_HEAD
    off = {name: i * hd for i, name in enumerate(("rq", "rk", "rv", "rz", "mq", "mk", "mv", "mo", "mz"))}
    g_off = 9 * hd
    n_g = 4 * N_HEADS
    off["gr"] = g_off + n_g
    off["gm"] = g_off + n_g + w_in.shape[0]
    widths = dict.fromkeys(off, hd)
    widths["gr"] = widths["gm"] = w_in.shape[0]

    def cols(a, name):
        return a[..., off[name]:off[name] + widths[name]]

    fm_names = ("rq", "rv", "rz", "mq", "mv", "mo", "mz", "gr", "gm")
    wt = jnp.concatenate([cols(w_in, n) for n in fm_names], axis=-1).T.astype(BF16)
    bt = jnp.concatenate([cols(b_in, n) for n in fm_names], axis=-1)[:, None]
    wk = jnp.concatenate([cols(w_in, "rk"), cols(w_in, "mk")], axis=-1).astype(BF16)
    bk = jnp.concatenate([cols(b_in, "rk"), cols(b_in, "mk")], axis=-1)[None, :]
    wg = w_in[:, g_off:g_off + n_g].reshape(-1, 4, N_HEADS).transpose(2, 1, 0)
    wg = jnp.pad(wg, ((0, 0), (0, N_GATE_ROWS - 4), (0, 0))).reshape(N_HEADS * N_GATE_ROWS, -1).astype(BF16)
    bg = b_in[g_off:g_off + n_g].reshape(4, N_HEADS).T
    bg = jnp.pad(bg, ((0, 0), (0, N_GATE_ROWS - 4))).reshape(N_HEADS * N_GATE_ROWS, 1)
    return wt, bt, wg, bg, wk, bk


def kernel(x, c, ctx, c_ctx, norm_w, w_ada, b_ada, w_in, b_in, conv_w, conv_b, ret_log_gamma, ret_norm_w,
           ml_norm_w, w_ret_o, w_ml_o, w_out, final_norm_w):
    b, n_lat, d = x.shape
    n_ctx = ctx.shape[1]
    depth = w_in.shape[0]
    hd = N_HEADS * D_HEAD
    assert d == hd and n_lat % CHUNK == 0 and n_ctx % 256 == 0 and n_lat % GRID_W == 0

    rows = -(-(b + 1) // 8) * 8
    cc = jnp.zeros((rows, d), F32).at[:b].set(c).at[b].set(c_ctx)
    mods = _adaln(cc, w_ada, b_ada)
    tables = _rope_tables(n_lat)
    xs = jnp.concatenate([ctx, x], axis=1)

    for l in range(depth):
        final = l == depth - 1
        mx = mods[l, :b].reshape(b, 3, d)
        mc = mods[l, b].reshape(3, d)
        wt, bt, wg, bg, wk, bk = _split_weights(w_in[l], b_in[l])
        ut, gt, uk = _proj(xs, norm_w[l][None, :], mx, mc, wt, bt, wg, bg, wk, bk, n_ctx)
        yr = _ret_mix(ret_log_gamma[l], ut, uk, tables, ret_norm_w[l][:, None], n_ctx, not final)
        cw = conv_w[l]
        ym = _ml_mix(ut, uk, gt, cw[:, :hd, None], conv_b[l][:hd, None], cw[:, hd:], conv_b[l][None, hd:],
                     ml_norm_w[l][:, None], n_ctx, not final)
        gates = jnp.stack([jnp.broadcast_to(mc[2], (b, d)), mx[:, 2]], axis=1)[:, :, None, :]
        xs = _out(yr, ym, ut, w_ret_o[l].T.astype(BF16), w_ml_o[l].T.astype(BF16), w_out[l].astype(BF16),
                  xs, gates, final_norm_w[None, :], n_ctx, final)
    return xs
```

```python
import functools

import jax
import jax.numpy as jnp
from jax import lax
from jax.experimental import pallas as pl
from jax.experimental.pallas import tpu as pltpu

F32 = jnp.float32
BF16 = jnp.bfloat16

N_HEADS = 8
D_HEAD = 128
CHUNK = 128
CONV_W = 5
GRID_W = 64
ROPE_BASE = 10000.0
EPS = 1e-6
NEG = -1e30
SUBLANES = 8
ONES_ROWS = 16
AUG = D_HEAD + ONES_ROWS

VMEM_LIMIT_BYTES = 56 * 1024 * 1024

NT_DIMS = (((1,), (1,)), ((), ()))
TN_DIMS = (((0,), (0,)), ((), ()))

FM_GROUPS = ("rq", "rv", "rz", "mv", "mo", "mz", "gr", "gm")
TM_GROUPS = ("rk", "mk", "mq")

GQ_ROWK, GQ_COLQ, GQ_SC, GQ_EMM, GQ_EW, GQ_DEC = range(6)
N_GQ = 12
GT_CM, GT_TOT, GT_RMAX, GT_BCUM = range(4)


def _silu(v):
    return v * jax.nn.sigmoid(v)


def _params(*sem):
    return pltpu.CompilerParams(dimension_semantics=sem, vmem_limit_bytes=VMEM_LIMIT_BYTES)


def _adaln_kernel(c_ref, w_ref, b_ref, o_ref):
    s = _silu(c_ref[...])
    o_ref[0] = jnp.dot(s, w_ref[0], preferred_element_type=F32,
                       precision=lax.Precision.HIGHEST) + b_ref[0]


def _adaln(cc, w_ada, b_ada):
    depth, d, d3 = w_ada.shape
    rows = cc.shape[0]
    tn = 512
    return pl.pallas_call(
        _adaln_kernel,
        grid=(depth, d3 // tn),
        in_specs=[pl.BlockSpec((rows, d), lambda l, j: (0, 0)),
                  pl.BlockSpec((1, d, tn), lambda l, j: (l, 0, j)),
                  pl.BlockSpec((1, 1, tn), lambda l, j: (l, 0, j))],
        out_specs=pl.BlockSpec((1, rows, tn), lambda l, j: (l, 0, j)),
        out_shape=jax.ShapeDtypeStruct((depth, rows, d3), F32),
        compiler_params=_params("parallel", "parallel"),
        name="adaln",
    )(cc, w_ada, b_ada.reshape(depth, 1, d3))


def _norm_mod_to_scratch(x_ref, nw_ref, mx_ref, mc_ref, h_scr, n_ctx):
    rows = 256
    t = x_ref.shape[1]

    def body(i, carry):
        r0 = pl.multiple_of(i * rows, rows)
        xv = x_ref[0, pl.ds(r0, rows), :]
        ms = jnp.mean(xv * xv, axis=-1, keepdims=True)
        y = xv * lax.rsqrt(ms + EPS) * nw_ref[...]
        is_ctx = i < n_ctx // rows
        sh = jnp.where(is_ctx, mc_ref[0:1, :], mx_ref[0, 0:1, :])
        sc = jnp.where(is_ctx, mc_ref[1:2, :], mx_ref[0, 1:2, :])
        h_scr[pl.ds(r0, rows), :] = (y * (1.0 + sc) + sh).astype(BF16)
        return carry

    lax.fori_loop(0, t // rows, body, 0)


def _proj_fm_kernel(x_ref, nw_ref, mx_ref, mc_ref, wt_ref, bt_ref, wg_ref, bg_ref,
                    ut_ref, g_ref, h_scr, *, n_ctx):
    @pl.when(pl.program_id(1) == 0)
    def _():
        _norm_mod_to_scratch(x_ref, nw_ref, mx_ref, mc_ref, h_scr, n_ctx)
        g = lax.dot_general(wg_ref[...], h_scr[...], NT_DIMS, preferred_element_type=F32)
        g_ref[0] = g + bg_ref[...]

    acc = lax.dot_general(wt_ref[...], h_scr[...], NT_DIMS, preferred_element_type=F32)
    ut_ref[0] = (acc + bt_ref[...]).astype(BF16)


def _proj_tm_kernel(x_ref, nw_ref, mx_ref, mc_ref, wk_ref, bk_ref, uk_ref, h_scr, *, n_ctx):
    @pl.when(pl.program_id(1) == 0)
    def _():
        _norm_mod_to_scratch(x_ref, nw_ref, mx_ref, mc_ref, h_scr, n_ctx)

    acc = jnp.dot(h_scr[...], wk_ref[...], preferred_element_type=F32)
    uk_ref[0] = (acc + bk_ref[...]).astype(BF16)


def _proj(xs, nw, mx, mc, wt, bt, wg, bg, wk, bk, n_ctx):
    b, t, d = xs.shape
    n_fm = wt.shape[0]
    n_tm = wk.shape[1]
    tn = 512
    common = [pl.BlockSpec((1, t, d), lambda i, j: (i, 0, 0)),
              pl.BlockSpec((1, d), lambda i, j: (0, 0)),
              pl.BlockSpec((1, 3, d), lambda i, j: (i, 0, 0)),
              pl.BlockSpec((3, d), lambda i, j: (0, 0))]
    ut, gt = pl.pallas_call(
        functools.partial(_proj_fm_kernel, n_ctx=n_ctx),
        grid=(b, n_fm // tn),
        in_specs=common + [pl.BlockSpec((tn, d), lambda i, j: (j, 0)),
                           pl.BlockSpec((tn, 1), lambda i, j: (j, 0)),
                           pl.BlockSpec(wg.shape, lambda i, j: (0, 0)),
                           pl.BlockSpec(bg.shape, lambda i, j: (0, 0))],
        out_specs=[pl.BlockSpec((1, tn, t), lambda i, j: (i, j, 0)),
                   pl.BlockSpec((1, wg.shape[0], t), lambda i, j: (i, 0, 0))],
        out_shape=[jax.ShapeDtypeStruct((b, n_fm, t), BF16),
                   jax.ShapeDtypeStruct((b, wg.shape[0], t), F32)],
        scratch_shapes=[pltpu.VMEM((t, d), BF16)],
        compiler_params=_params("parallel", "arbitrary"),
        name="proj_fm",
    )(xs, nw, mx, mc, wt, bt, wg, bg)
    uk = pl.pallas_call(
        functools.partial(_proj_tm_kernel, n_ctx=n_ctx),
        grid=(b, n_tm // tn),
        in_specs=common + [pl.BlockSpec((d, tn), lambda i, j: (0, j)),
                           pl.BlockSpec((1, tn), lambda i, j: (0, j))],
        out_specs=pl.BlockSpec((1, t, tn), lambda i, j: (i, 0, j)),
        out_shape=jax.ShapeDtypeStruct((b, t, n_tm), BF16),
        scratch_shapes=[pltpu.VMEM((t, d), BF16)],
        compiler_params=_params("parallel", "arbitrary"),
        name="proj_tm",
    )(xs, nw, mx, mc, wk, bk)
    return ut, gt, uk


def _chunk_start(c):
    return pl.multiple_of(c * CHUNK, CHUNK)


def _loop(lo, hi, body, reverse=False, unroll=1):
    n = hi - lo
    if n <= 0:
        return

    def wrapped(i, carry):
        body(hi - 1 - i if reverse else lo + i)
        return carry

    lax.fori_loop(0, n, wrapped, 0, unroll=unroll)


def _unroll(n):
    return max(k for k in (1, 2, 3, 4) if n % k == 0)


def _scan(lo, hi, body, init, reverse=False):
    n = hi - lo
    if n <= 0:
        return init

    def wrapped(i, carry):
        return body(hi - 1 - i if reverse else lo + i, carry)

    return lax.fori_loop(0, n, wrapped, init)


def _head_norm_gate(o, nw_col, z):
    mu = jnp.mean(o, axis=0, keepdims=True)
    oc = o - mu
    var = jnp.mean(oc * oc, axis=0, keepdims=True)
    return oc * lax.rsqrt(var + EPS) * nw_col * _silu(z)


def _ret_kernel(lg_ref, qt_ref, k_ref, vt_ref, zt_ref, cos_t_ref, sin_t_ref, cos_k_ref, sin_k_ref,
                nw_ref, y_ref, qs, ks, kv, st, pt_scr, cr_scr, *, n_ctx_chunks, n_chunks, out_ctx):
    c_ = CHUNK
    h = pl.program_id(1)
    lg_f = lg_ref[0, h]
    lg_b = lg_ref[1, h]
    lane = lax.broadcasted_iota(jnp.int32, (1, c_), 1).astype(F32)
    zeta_f = jnp.exp(lg_f * (c_ - 1.0 - lane)).astype(BF16)
    zeta_b = jnp.exp(lg_b * lane).astype(BF16)
    xi_f = jnp.exp(lg_f * (lane + 1.0))
    xi_b = jnp.exp(lg_b * (c_ - lane))
    dec_f = jnp.exp(lg_f * jnp.full((1, c_), float(c_), F32))
    dec_b = jnp.exp(lg_b * jnp.full((1, c_), float(c_), F32))
    ki = lax.broadcasted_iota(jnp.int32, (c_, c_), 0)
    qi = lax.broadcasted_iota(jnp.int32, (c_, c_), 1)
    diff = (qi - ki).astype(F32)
    dsum = (jnp.where(diff >= 0, jnp.exp(lg_f * jnp.maximum(diff, 0.0)), 0.0)
            + jnp.where(diff <= 0, jnp.exp(lg_b * jnp.maximum(-diff, 0.0)), 0.0))
    scale = D_HEAD ** -0.5

    def prep_ctx(c):
        t0 = _chunk_start(c)
        qs[:, pl.ds(t0, c_)] = (qt_ref[0, :, pl.ds(t0, c_)].astype(F32) * scale).astype(BF16)
        ks[pl.ds(t0, c_), :] = k_ref[0, pl.ds(t0, c_), :]

    def prep_lat(c):
        t0 = _chunk_start(c)
        l0 = _chunk_start(c - n_ctx_chunks)
        q = qt_ref[0, :, pl.ds(t0, c_)].astype(F32)
        k = k_ref[0, pl.ds(t0, c_), :].astype(F32)
        qr = q * cos_t_ref[:, pl.ds(l0, c_)] + pltpu.roll(q, D_HEAD // 2, 0) * sin_t_ref[:, pl.ds(l0, c_)]
        kr = k * cos_k_ref[pl.ds(l0, c_), :] + pltpu.roll(k, D_HEAD // 2, 1) * sin_k_ref[pl.ds(l0, c_), :]
        qs[:, pl.ds(t0, c_)] = (qr * scale).astype(BF16)
        ks[pl.ds(t0, c_), :] = kr.astype(BF16)

    _loop(0, n_ctx_chunks, prep_ctx)
    _loop(n_ctx_chunks, n_chunks, prep_lat, unroll=2)

    def outer_products(c):
        t0 = _chunk_start(c)
        vt = vt_ref[0, :, pl.ds(t0, c_)]
        lhs = jnp.concatenate([vt * zeta_f, vt * zeta_b], axis=0)
        kv[c] = jnp.dot(lhs, ks[pl.ds(t0, c_), :], preferred_element_type=F32)

    _loop(0, n_chunks, outer_products, unroll=2)

    def scan_f(c, s):
        st[c, 0:c_, :] = s.astype(BF16)
        return dec_f * s + kv[c, 0:c_, :]

    def scan_b(c, s):
        st[c, c_:2 * c_, :] = s.astype(BF16)
        return dec_b * s + kv[c, c_:2 * c_, :]

    zero = jnp.zeros((c_, c_), F32)
    _scan(0, n_chunks, scan_f, zero)
    sb = _scan(0, n_ctx_chunks, scan_b, zero, reverse=True)
    _scan(n_ctx_chunks, n_chunks, scan_b, sb, reverse=True)

    nw_col = nw_ref[...]

    def scores(c):
        t0 = _chunk_start(c)
        qt = qs[:, pl.ds(t0, c_)]
        at = jnp.dot(ks[pl.ds(t0, c_), :], qt, preferred_element_type=F32)
        crs = jnp.dot(st[c], qt, preferred_element_type=F32)
        pt_scr[c] = (at * dsum).astype(BF16)
        cr_scr[c] = crs[0:c_, :] * xi_f + crs[c_:2 * c_, :] * xi_b

    def outputs(c):
        t0 = _chunk_start(c)
        o = jnp.dot(vt_ref[0, :, pl.ds(t0, c_)], pt_scr[c], preferred_element_type=F32) + cr_scr[c]
        z = zt_ref[0, :, pl.ds(t0, c_)].astype(F32)
        y_ref[0, :, pl.ds(t0, c_)] = _head_norm_gate(o, nw_col, z).astype(BF16)

    lo = 0 if out_ctx else n_ctx_chunks
    if not out_ctx:
        y_ref[0, :, 0:n_ctx_chunks * c_] = jnp.zeros((D_HEAD, n_ctx_chunks * c_), BF16)
    _loop(lo, n_chunks, scores, unroll=_unroll(n_chunks - lo))
    _loop(lo, n_chunks, outputs, unroll=_unroll(n_chunks - lo))


def _ret_mix(lg, ut, uk, tables, nw, n_ctx, out_ctx):
    b, _, t = ut.shape
    n_chunks = t // CHUNK
    hd = D_HEAD
    cos_t, sin_t, cos_k, sin_k = tables
    n_lat = cos_k.shape[0]

    def fm(name):
        g = FM_GROUPS.index(name)
        return pl.BlockSpec((1, hd, t), lambda i, h: (i, g * N_HEADS + h, 0))

    tm_rk = TM_GROUPS.index("rk")
    kern = functools.partial(_ret_kernel, n_ctx_chunks=n_ctx // CHUNK, n_chunks=n_chunks, out_ctx=out_ctx)
    return pl.pallas_call(
        kern,
        grid=(b, N_HEADS),
        in_specs=[pl.BlockSpec(memory_space=pltpu.SMEM),
                  fm("rq"),
                  pl.BlockSpec((1, t, hd), lambda i, h: (i, 0, tm_rk * N_HEADS + h)),
                  fm("rv"), fm("rz"),
                  pl.BlockSpec((hd, n_lat), lambda i, h: (0, 0)),
                  pl.BlockSpec((hd, n_lat), lambda i, h: (0, 0)),
                  pl.BlockSpec((n_lat, hd), lambda i, h: (0, 0)),
                  pl.BlockSpec((n_lat, hd), lambda i, h: (0, 0)),
                  pl.BlockSpec((hd, 1), lambda i, h: (h, 0))],
        out_specs=pl.BlockSpec((1, hd, t), lambda i, h: (i, h, 0)),
        out_shape=jax.ShapeDtypeStruct((b, N_HEADS * hd, t), BF16),
        scratch_shapes=[pltpu.VMEM((hd, t), BF16),
                        pltpu.VMEM((t, hd), BF16),
                        pltpu.VMEM((n_chunks, 2 * hd, hd), F32),
                        pltpu.VMEM((n_chunks, 2 * hd, hd), BF16),
                        pltpu.VMEM((n_chunks, hd, hd), BF16),
                        pltpu.VMEM((n_chunks, hd, hd), F32)],
        compiler_params=_params("parallel", "arbitrary"),
        name="ret_mix",
    )(lg, ut, uk, ut, ut, cos_t, sin_t, cos_k, sin_k, nw)


def _seg_scan(v, pos, op, fill, reverse):
    t = v.shape[1]
    step = 1
    while step < CHUNK:
        if reverse:
            v = op(v, jnp.where(pos < CHUNK - step, pltpu.roll(v, t - step, 1), fill))
        else:
            v = op(v, jnp.where(pos >= step, pltpu.roll(v, step, 1), fill))
        step *= 2
    return v


def _gate_kernel(g_ref, o_ref, tmp, *, n_ctx_chunks, n_chunks):
    c_ = CHUNK
    nh = N_HEADS
    g = g_ref[0]
    pos = lax.broadcasted_iota(jnp.int32, (nh, g.shape[1]), 1) & (c_ - 1)
    for d in range(2):
        rev = d == 1
        ig = g[2 * d * nh:(2 * d + 1) * nh]
        lf = jax.nn.log_sigmoid(g[(2 * d + 1) * nh:(2 * d + 2) * nh])
        bcum = _seg_scan(lf, pos, jnp.add, 0.0, rev)
        tot = bcum + _seg_scan(lf, pos, jnp.add, 0.0, not rev) - lf
        r = ig - bcum
        cm = _seg_scan(r, pos, jnp.maximum, NEG, rev)
        rmax = jnp.maximum(cm, _seg_scan(r, pos, jnp.maximum, NEG, not rev))
        o_ref[0, 2 * GQ_ROWK + d] = r
        tmp[2 * GT_CM + d] = cm
        tmp[2 * GT_TOT + d] = tot
        tmp[2 * GT_RMAX + d] = rmax
        tmp[2 * GT_BCUM + d] = bcum

    def make_step(d):
        def step(c, m_prev):
            sl = pl.ds(_chunk_start(c), c_)
            gt = tmp[2 * GT_TOT + d, :, sl]
            m_new = jnp.maximum(gt + m_prev, gt + tmp[2 * GT_RMAX + d, :, sl])
            mx = jnp.maximum(tmp[2 * GT_CM + d, :, sl], m_prev)
            o_ref[0, 2 * GQ_COLQ + d, :, sl] = -mx
            o_ref[0, 2 * GQ_SC + d, :, sl] = jnp.exp(m_prev - mx)
            o_ref[0, 2 * GQ_EMM + d, :, sl] = jnp.exp(-(tmp[2 * GT_BCUM + d, :, sl] + mx))
            o_ref[0, 2 * GQ_EW + d, :, sl] = jnp.exp(gt + o_ref[0, 2 * GQ_ROWK + d, :, sl] - m_new)
            o_ref[0, 2 * GQ_DEC + d, :, sl] = jnp.exp(gt + m_prev - m_new)
            return m_new
        return step

    m0 = jnp.zeros((nh, c_), F32)
    _scan(0, n_chunks, make_step(0), m0)
    mb = _scan(0, n_ctx_chunks, make_step(1), m0, reverse=True)
    _scan(n_ctx_chunks, n_chunks, make_step(1), mb, reverse=True)


def _gate_rows(gt, n_ctx):
    b, n_rows, t = gt.shape
    kern = functools.partial(_gate_kernel, n_ctx_chunks=n_ctx // CHUNK, n_chunks=t // CHUNK)
    rows = pl.pallas_call(
        kern,
        grid=(b,),
        in_specs=[pl.BlockSpec((1, n_rows, t), lambda i: (i, 0, 0))],
        out_specs=pl.BlockSpec((1, N_GQ, N_HEADS, t), lambda i: (i, 0, 0, 0)),
        out_shape=jax.ShapeDtypeStruct((b, N_GQ, N_HEADS, t), F32),
        scratch_shapes=[pltpu.VMEM((8, N_HEADS, t), F32)],
        compiler_params=_params("parallel"),
        name="ml_gates",
    )(gt)
    return rows.transpose(0, 2, 1, 3)


def _shifted_rows(prev8, cur, nxt8, s):
    n = cur.shape[0]
    row = lax.broadcasted_iota(jnp.int32, (SUBLANES, cur.shape[1]), 0)
    if s > 0:
        rc = pltpu.roll(cur, n - s, 0)
        edge = jnp.where(row < SUBLANES - s, rc[n - SUBLANES:], pltpu.roll(nxt8, SUBLANES - s, 0))
        return jnp.concatenate([rc[:n - SUBLANES], edge], axis=0)
    rc = pltpu.roll(cur, -s, 0)
    edge = jnp.where(row >= -s, rc[:SUBLANES], pltpu.roll(prev8, -s, 0))
    return jnp.concatenate([edge, rc[SUBLANES:]], axis=0)


def _ml_kernel(q_ref, k_ref, vt_ref, ot_ref, zt_ref, gs_ref, cwq_ref, cbq_ref, cwk_ref, cbk_ref, nw_ref,
               y_ref, qs, ks, kv, st, pt_scr, cr_scr, *, n_ctx_chunks, n_chunks, out_ctx):
    c_ = CHUNK
    scale = D_HEAD ** -0.5
    half = CONV_W // 2
    pad = 2 * SUBLANES

    def grow(q, d, sl):
        return gs_ref[0, 0, 2 * q + d:2 * q + d + 1, sl]

    def conv_segment(lo, hi):
        def conv_one(ref, w_ref, b_ref, c):
            t0 = _chunk_start(c)
            tp = pl.multiple_of(jnp.maximum(c - 1, lo) * c_ + c_ - pad, pad)
            tn = pl.multiple_of(jnp.minimum(c + 1, hi - 1) * c_, pad)
            cur = ref[0, pl.ds(t0, c_), :].astype(F32)
            prev8 = jnp.where(c > lo, ref[0, pl.ds(tp, pad), :].astype(F32)[pad - SUBLANES:], 0.0)
            nxt8 = jnp.where(c < hi - 1, ref[0, pl.ds(tn, pad), :].astype(F32)[:SUBLANES], 0.0)
            acc = cur * w_ref[half:half + 1, :] + b_ref[...]
            for j in range(CONV_W):
                if j != half:
                    acc = acc + _shifted_rows(prev8, cur, nxt8, j - half) * w_ref[j:j + 1, :]
            return _silu(acc)

        def body(c):
            t0 = _chunk_start(c)
            qs[pl.ds(t0, c_), :] = (conv_one(q_ref, cwq_ref, cbq_ref, c) * scale).astype(BF16)
            ks[pl.ds(t0, c_), :] = conv_one(k_ref, cwk_ref, cbk_ref, c).astype(BF16)
        _loop(lo, hi, body, unroll=2)

    conv_segment(0, n_ctx_chunks)
    conv_segment(n_ctx_chunks, n_chunks)

    ones = jnp.ones((ONES_ROWS, c_), BF16)

    def outer_products(c):
        sl = pl.ds(_chunk_start(c), c_)
        vt = vt_ref[0, :, sl]
        ew_f = grow(GQ_EW, 0, sl).astype(BF16)
        ew_b = grow(GQ_EW, 1, sl).astype(BF16)
        lhs = jnp.concatenate([vt * ew_f, ones * ew_f, vt * ew_b, ones * ew_b], axis=0)
        kv[c] = jnp.dot(lhs, ks[sl, :], preferred_element_type=F32)

    _loop(0, n_chunks, outer_products, unroll=2)

    def make_scan(d):
        def step(c, s):
            sl = pl.ds(_chunk_start(c), c_)
            st[c, d * AUG:(d + 1) * AUG, :] = s.astype(BF16)
            return grow(GQ_DEC, d, sl) * s + kv[c, d * AUG:(d + 1) * AUG, :]
        return step

    zero = jnp.zeros((AUG, c_), F32)
    _scan(0, n_chunks, make_scan(0), zero)
    sb = _scan(0, n_ctx_chunks, make_scan(1), zero, reverse=True)
    _scan(n_ctx_chunks, n_chunks, make_scan(1), sb, reverse=True)

    nw_col = nw_ref[...]
    ki = lax.broadcasted_iota(jnp.int32, (c_, c_), 0)
    qi = lax.broadcasted_iota(jnp.int32, (c_, c_), 1)

    def scores(c):
        sl = pl.ds(_chunk_start(c), c_)
        qc = qs[sl, :]
        at = lax.dot_general(ks[sl, :], qc, NT_DIMS, preferred_element_type=F32)
        crs = lax.dot_general(st[c], qc, NT_DIMS, preferred_element_type=F32)
        rowk_f = jnp.broadcast_to(grow(GQ_ROWK, 0, sl), (c_, c_)).T
        rowk_b = jnp.broadcast_to(grow(GQ_ROWK, 1, sl), (c_, c_)).T
        s_f = at * jnp.exp(jnp.where(ki <= qi, rowk_f + grow(GQ_COLQ, 0, sl), NEG))
        s_b = at * jnp.exp(jnp.where(ki >= qi, rowk_b + grow(GQ_COLQ, 1, sl), NEG))
        sc_f = grow(GQ_SC, 0, sl)
        sc_b = grow(GQ_SC, 1, sl)
        den_f = jnp.sum(s_f, axis=0, keepdims=True) + crs[D_HEAD:D_HEAD + 1, :] * sc_f
        den_b = jnp.sum(s_b, axis=0, keepdims=True) + crs[AUG + D_HEAD:AUG + D_HEAD + 1, :] * sc_b
        r_f = 1.0 / jnp.maximum(jnp.abs(den_f), grow(GQ_EMM, 0, sl))
        r_b = 1.0 / jnp.maximum(jnp.abs(den_b), grow(GQ_EMM, 1, sl))
        pt_scr[c] = (s_f * r_f + s_b * r_b).astype(BF16)
        cr_scr[c] = crs[0:D_HEAD, :] * (sc_f * r_f) + crs[AUG:AUG + D_HEAD, :] * (sc_b * r_b)

    def outputs(c):
        sl = pl.ds(_chunk_start(c), c_)
        o = jnp.dot(vt_ref[0, :, sl], pt_scr[c], preferred_element_type=F32) + cr_scr[c]
        o = jax.nn.sigmoid(ot_ref[0, :, sl].astype(F32)) * o
        z = zt_ref[0, :, sl].astype(F32)
        y_ref[0, :, sl] = _head_norm_gate(o, nw_col, z).astype(BF16)

    lo = 0 if out_ctx else n_ctx_chunks
    if not out_ctx:
        y_ref[0, :, 0:n_ctx_chunks * c_] = jnp.zeros((D_HEAD, n_ctx_chunks * c_), BF16)
    _loop(lo, n_chunks, scores, unroll=_unroll(n_chunks - lo))
    _loop(lo, n_chunks, outputs, unroll=_unroll(n_chunks - lo))


def _ml_mix(ut, uk, gs, cwq, cbq, cwk, cbk, nw, n_ctx, out_ctx):
    b, _, t = ut.shape
    n_chunks = t // CHUNK
    hd = D_HEAD

    def fm(name):
        g = FM_GROUPS.index(name)
        return pl.BlockSpec((1, hd, t), lambda i, h: (i, g * N_HEADS + h, 0))

    def tm(name):
        g = TM_GROUPS.index(name)
        return pl.BlockSpec((1, t, hd), lambda i, h: (i, 0, g * N_HEADS + h))

    kern = functools.partial(_ml_kernel, n_ctx_chunks=n_ctx // CHUNK, n_chunks=n_chunks, out_ctx=out_ctx)
    return pl.pallas_call(
        kern,
        grid=(b, N_HEADS),
        in_specs=[tm("mq"), tm("mk"), fm("mv"), fm("mo"), fm("mz"),
                  pl.BlockSpec((1, 1, N_GQ, t), lambda i, h: (i, h, 0, 0)),
                  pl.BlockSpec((CONV_W, hd), lambda i, h: (0, h)),
                  pl.BlockSpec((1, hd), lambda i, h: (0, h)),
                  pl.BlockSpec((CONV_W, hd), lambda i, h: (0, h)),
                  pl.BlockSpec((1, hd), lambda i, h: (0, h)),
                  pl.BlockSpec((hd, 1), lambda i, h: (h, 0))],
        out_specs=pl.BlockSpec((1, hd, t), lambda i, h: (i, h, 0)),
        out_shape=jax.ShapeDtypeStruct((b, N_HEADS * hd, t), BF16),
        scratch_shapes=[pltpu.VMEM((t, hd), BF16),
                        pltpu.VMEM((t, hd), BF16),
                        pltpu.VMEM((n_chunks, 2 * AUG, hd), F32),
                        pltpu.VMEM((n_chunks, 2 * AUG, hd), BF16),
                        pltpu.VMEM((n_chunks, hd, hd), BF16),
                        pltpu.VMEM((n_chunks, hd, hd), F32)],
        compiler_params=_params("parallel", "arbitrary"),
        name="ml_mix",
    )(uk, uk, ut, ut, ut, gs, cwq, cbq, cwk, cbk, nw)


def _out_kernel(yr_ref, ym_ref, gr_ref, gm_ref, wro_ref, wmo_ref, wout_ref, x_ref, gate_ref, fnw_ref,
                o_ref, *, final):
    br = jnp.dot(wro_ref[...], yr_ref[0], preferred_element_type=F32)
    bm = jnp.dot(wmo_ref[...], ym_ref[0], preferred_element_type=F32)
    y = (jax.nn.sigmoid(gr_ref[0].astype(F32)) * br + jax.nn.sigmoid(gm_ref[0].astype(F32)) * bm).astype(BF16)
    o = lax.dot_general(y, wout_ref[...], TN_DIMS, preferred_element_type=F32)
    xn = x_ref[0] + gate_ref[0, 0] * o
    if final:
        ms = jnp.mean(xn * xn, axis=-1, keepdims=True)
        xn = xn * lax.rsqrt(ms + EPS) * fnw_ref[...]
    o_ref[0] = xn


def _out(yr, ym, ut, wro_t, wmo_t, wout, xs, gates, fnw, n_ctx, final):
    b, t, d = xs.shape
    tm = 256
    skip = n_ctx // tm if final else 0
    n_tiles = t // tm - skip
    ctx_tiles = n_ctx // tm
    gr_blk = FM_GROUPS.index("gr") * N_HEADS * D_HEAD // d
    gm_blk = FM_GROUPS.index("gm") * N_HEADS * D_HEAD // d
    kern = functools.partial(_out_kernel, final=final)
    return pl.pallas_call(
        kern,
        grid=(b, n_tiles),
        in_specs=[pl.BlockSpec((1, d, tm), lambda i, j: (i, 0, j + skip)),
                  pl.BlockSpec((1, d, tm), lambda i, j: (i, 0, j + skip)),
                  pl.BlockSpec((1, d, tm), lambda i, j: (i, gr_blk, j + skip)),
                  pl.BlockSpec((1, d, tm), lambda i, j: (i, gm_blk, j + skip)),
                  pl.BlockSpec((d, d), lambda i, j: (0, 0)),
                  pl.BlockSpec((d, d), lambda i, j: (0, 0)),
                  pl.BlockSpec((d, d), lambda i, j: (0, 0)),
                  pl.BlockSpec((1, tm, d), lambda i, j: (i, j + skip, 0)),
                  pl.BlockSpec((1, 1, 1, d), lambda i, j: (i, jnp.where(j + skip < ctx_tiles, 0, 1), 0, 0)),
                  pl.BlockSpec((1, d), lambda i, j: (0, 0))],
        out_specs=pl.BlockSpec((1, tm, d), lambda i, j: (i, j, 0)),
        out_shape=jax.ShapeDtypeStruct((b, n_tiles * tm, d), F32),
        compiler_params=_params("parallel", "parallel"),
        name="out_final" if final else "out",
    )(yr, ym, ut, ut, wro_t, wmo_t, wout, xs, gates, fnw)


def _rope_tables(n_lat):
    rows_n = n_lat // GRID_W
    rows = jnp.repeat(jnp.arange(rows_n, dtype=F32), GRID_W)
    cols = jnp.tile(jnp.arange(GRID_W, dtype=F32), rows_n)
    nf = D_HEAD // 4
    freqs = ROPE_BASE ** (-jnp.arange(nf, dtype=F32) / nf)
    ang = jnp.concatenate([rows[:, None] * freqs, cols[:, None] * freqs], axis=-1)
    cos, sin = jnp.cos(ang), jnp.sin(ang)
    cos_k = jnp.concatenate([cos, cos], axis=-1)
    sin_k = jnp.concatenate([-sin, sin], axis=-1)
    return cos_k.T, sin_k.T, cos_k, sin_k


def _split_weights(w_in, b_in):
    hd = N_HEADS * D_HEAD
    d = w_in.shape[0]
    off = {name: i * hd for i, name in enumerate(("rq", "rk", "rv", "rz", "mq", "mk", "mv", "mo", "mz"))}
    g_off = 9 * hd
    n_g = 4 * N_HEADS
    off["gr"] = g_off + n_g
    off["gm"] = g_off + n_g + d
    widths = dict.fromkeys(off, hd)
    widths["gr"] = widths["gm"] = d

    def cols(a, name):
        return a[..., off[name]:off[name] + widths[name]]

    wt = jnp.concatenate([cols(w_in, n) for n in FM_GROUPS], axis=-1).T.astype(BF16)
    bt = jnp.concatenate([cols(b_in, n) for n in FM_GROUPS], axis=-1)[:, None]
    wk = jnp.concatenate([cols(w_in, n) for n in TM_GROUPS], axis=-1).astype(BF16)
    bk = jnp.concatenate([cols(b_in, n) for n in TM_GROUPS], axis=-1)[None, :]
    wg = w_in[:, g_off:g_off + n_g].T.astype(BF16)
    bg = b_in[g_off:g_off + n_g][:, None]
    return wt, bt, wg, bg, wk, bk


def kernel(x, c, ctx, c_ctx, norm_w, w_ada, b_ada, w_in, b_in, conv_w, conv_b, ret_log_gamma, ret_norm_w,
           ml_norm_w, w_ret_o, w_ml_o, w_out, final_norm_w):
    b, n_lat, d = x.shape
    n_ctx = ctx.shape[1]
    depth = w_in.shape[0]
    hd = N_HEADS * D_HEAD
    assert d == hd and n_lat % CHUNK == 0 and n_ctx % 256 == 0 and n_lat % GRID_W == 0

    rows = -(-(b + 1) // SUBLANES) * SUBLANES
    cc = jnp.zeros((rows, d), F32).at[:b].set(c).at[b].set(c_ctx)
    mods = _adaln(cc, w_ada, b_ada)
    tables = _rope_tables(n_lat)
    xs = jnp.concatenate([ctx, x], axis=1)

    for l in range(depth):
        final = l == depth - 1
        mx = mods[l, :b].reshape(b, 3, d)
        mc = mods[l, b].reshape(3, d)
        wt, bt, wg, bg, wk, bk = _split_weights(w_in[l], b_in[l])
        ut, gt, uk = _proj(xs, norm_w[l][None, :], mx, mc, wt, bt, wg, bg, wk, bk, n_ctx)
        yr = _ret_mix(ret_log_gamma[l], ut, uk, tables, ret_norm_w[l][:, None], n_ctx, not final)
        gs = _gate_rows(gt, n_ctx)
        cw = conv_w[l]
        ym = _ml_mix(ut, uk, gs, cw[:, :hd], conv_b[l][None, :hd], cw[:, hd:], conv_b[l][None, hd:],
                     ml_norm_w[l][:, None], n_ctx, not final)
        gates = jnp.stack([jnp.broadcast_to(mc[2], (b, d)), mx[:, 2]], axis=1)[:, :, None, :]
        xs = _out(yr, ym, ut, w_ret_o[l].T.astype(BF16), w_ml_o[l].T.astype(BF16), w_out[l].astype(BF16),
                  xs, gates, final_norm_w[None, :], n_ctx, final)
    return xs
```

```python
import functools

import jax
import jax.numpy as jnp
from jax import lax
from jax.experimental import pallas as pl
from jax.experimental.pallas import tpu as pltpu

F32 = jnp.float32
BF16 = jnp.bfloat16

N_HEADS = 8
D_HEAD = 128
CHUNK = 128
CONV_W = 5
GRID_W = 64
ROPE_BASE = 10000.0
EPS = 1e-6
NEG = -1e30
SUBLANES = 8
ONES_ROWS = 16
AUG = D_HEAD + ONES_ROWS
ROW_BLOCK = 256

VMEM_LIMIT_BYTES = 56 * 1024 * 1024

NT_DIMS = (((1,), (1,)), ((), ()))
TN_DIMS = (((0,), (0,)), ((), ()))

FM_GROUPS = ("rq", "rv", "rz", "mv", "mo", "mz", "gr", "gm")
TM_GROUPS = ("rk", "mk", "mq")

GQ_ROWK, GQ_COLQ, GQ_SC, GQ_EMM, GQ_EW, GQ_DEC = range(6)
N_GQ = 12
GT_CM, GT_TOT, GT_RMAX, GT_BCUM = range(4)


def _silu(v):
    return v * jax.nn.sigmoid(v)


def _params(*sem):
    return pltpu.CompilerParams(dimension_semantics=sem, vmem_limit_bytes=VMEM_LIMIT_BYTES)


def _adaln_kernel(c_ref, w_ref, b_ref, o_ref):
    s = _silu(c_ref[...])
    o_ref[0] = jnp.dot(s, w_ref[0], preferred_element_type=F32,
                       precision=lax.Precision.HIGHEST) + b_ref[0]


def _adaln(cc, w_ada, b_ada):
    depth, d, d3 = w_ada.shape
    rows = cc.shape[0]
    tn = 512
    return pl.pallas_call(
        _adaln_kernel,
        grid=(depth, d3 // tn),
        in_specs=[pl.BlockSpec((rows, d), lambda l, j: (0, 0)),
                  pl.BlockSpec((1, d, tn), lambda l, j: (l, 0, j)),
                  pl.BlockSpec((1, 1, tn), lambda l, j: (l, 0, j))],
        out_specs=pl.BlockSpec((1, rows, tn), lambda l, j: (l, 0, j)),
        out_shape=jax.ShapeDtypeStruct((depth, rows, d3), F32),
        compiler_params=_params("parallel", "parallel"),
        name="adaln",
    )(cc, w_ada, b_ada.reshape(depth, 1, d3))


def _norm_mod_to_scratch(ctx_ref, x_ref, nw_ref, mx_ref, mc_ref, h_scr):
    rows = ROW_BLOCK
    n_ctx = ctx_ref.shape[1]

    def block(src_ref, shift, scale, src_row, dst_row):
        xv = src_ref[0, pl.ds(src_row, rows), :]
        ms = jnp.mean(xv * xv, axis=-1, keepdims=True)
        y = xv * lax.rsqrt(ms + EPS) * nw_ref[...]
        h_scr[pl.ds(dst_row, rows), :] = (y * (1.0 + scale) + shift).astype(BF16)

    def ctx_body(i, carry):
        r0 = pl.multiple_of(i * rows, rows)
        block(ctx_ref, mc_ref[0:1, :], mc_ref[1:2, :], r0, r0)
        return carry

    def lat_body(i, carry):
        r0 = pl.multiple_of(i * rows, rows)
        block(x_ref, mx_ref[0, 0:1, :], mx_ref[0, 1:2, :], r0, pl.multiple_of(r0 + n_ctx, rows))
        return carry

    lax.fori_loop(0, n_ctx // rows, ctx_body, 0)
    lax.fori_loop(0, x_ref.shape[1] // rows, lat_body, 0)


def _proj_fm_kernel(ctx_ref, x_ref, nw_ref, mx_ref, mc_ref, wt_ref, bt_ref, wg_ref, bg_ref, cos_ref, sin_ref,
                    ut_ref, g_ref, h_scr, *, n_rope_tiles):
    j = pl.program_id(1)
    n_ctx = ctx_ref.shape[1]
    tn = wt_ref.shape[0]
    t = h_scr.shape[0]
    rb = ROW_BLOCK

    @pl.when(j == 0)
    def _():
        _norm_mod_to_scratch(ctx_ref, x_ref, nw_ref, mx_ref, mc_ref, h_scr)
        g = lax.dot_general(wg_ref[...], h_scr[...], NT_DIMS, preferred_element_type=F32)
        g_ref[0] = g + bg_ref[...]

    @pl.when(j < n_rope_tiles)
    def _():
        scale = D_HEAD ** -0.5
        for r0 in range(0, t, rb):
            acc = lax.dot_general(wt_ref[...], h_scr[r0:r0 + rb, :], NT_DIMS,
                                  preferred_element_type=F32) + bt_ref[...]
            if r0 >= n_ctx:
                l0 = r0 - n_ctx
                cos, sin = cos_ref[:, l0:l0 + rb], sin_ref[:, l0:l0 + rb]
                heads = [acc[hh:hh + D_HEAD] for hh in range(0, tn, D_HEAD)]
                acc = jnp.concatenate([q * cos + pltpu.roll(q, D_HEAD // 2, 0) * sin for q in heads], axis=0)
            ut_ref[0, :, r0:r0 + rb] = (acc * scale).astype(BF16)

    @pl.when(j >= n_rope_tiles)
    def _():
        acc = lax.dot_general(wt_ref[...], h_scr[...], NT_DIMS, preferred_element_type=F32)
        ut_ref[0] = (acc + bt_ref[...]).astype(BF16)


def _proj_tm_kernel(ctx_ref, x_ref, nw_ref, mx_ref, mc_ref, wk_ref, bk_ref, cw_ref, cb_ref, ps_ref, cos_ref, sin_ref,
                    uk_ref, h_scr, *, n_rope_tiles):
    j = pl.program_id(1)
    n_ctx = ctx_ref.shape[1]
    tn = wk_ref.shape[1]
    t = h_scr.shape[0]
    rb = ROW_BLOCK
    half = CONV_W // 2

    @pl.when(j == 0)
    def _():
        _norm_mod_to_scratch(ctx_ref, x_ref, nw_ref, mx_ref, mc_ref, h_scr)

    def project(r0):
        return jnp.dot(h_scr[r0:r0 + rb, :], wk_ref[...], preferred_element_type=F32) + bk_ref[...]

    @pl.when(j < n_rope_tiles)
    def _():
        for r0 in range(0, t, rb):
            acc = project(r0)
            if r0 >= n_ctx:
                l0 = r0 - n_ctx
                cos, sin = cos_ref[l0:l0 + rb, :], sin_ref[l0:l0 + rb, :]
                heads = [acc[:, hh:hh + D_HEAD] for hh in range(0, tn, D_HEAD)]
                acc = jnp.concatenate([k * cos + pltpu.roll(k, D_HEAD // 2, 1) * sin for k in heads], axis=1)
            uk_ref[0, r0:r0 + rb, :] = acc.astype(BF16)

    @pl.when(j >= n_rope_tiles)
    def _():
        zeros = jnp.zeros((SUBLANES, tn), F32)
        seg_edges = (0, n_ctx, t)
        raw = {}

        def conv_block(r0):
            top = zeros if r0 in seg_edges else raw[r0 - rb][rb - SUBLANES:]
            bot = zeros if r0 + rb in seg_edges else raw[r0 + rb][:SUBLANES]
            cur = raw[r0]
            ext = jnp.concatenate([top, cur, bot], axis=0)
            n_ext = rb + 2 * SUBLANES
            acc = cur * cw_ref[half:half + 1, :] + cb_ref[...]
            for jj in range(CONV_W):
                s = jj - half
                if s != 0:
                    shifted = pltpu.roll(ext, (-s) % n_ext, 0)[SUBLANES:SUBLANES + rb]
                    acc = acc + shifted * cw_ref[jj:jj + 1, :]
            uk_ref[0, r0:r0 + rb, :] = (_silu(acc) * ps_ref[...]).astype(BF16)

        for r0 in range(0, t, rb):
            raw[r0] = project(r0)
            if r0 > 0:
                conv_block(r0 - rb)
        conv_block(t - rb)


def _proj(ctx, x, nw, mx, mc, w, tables):
    b, n_lat, d = x.shape
    n_ctx = ctx.shape[1]
    t = n_ctx + n_lat
    wt, bt, wg, bg, wk, bk, cw, cb, ps = w
    cos_t, sin_t, cos_k, sin_k = tables
    n_fm = wt.shape[0]
    n_tm = wk.shape[1]
    tn = 512
    rope_tiles = N_HEADS * D_HEAD // tn
    common = [pl.BlockSpec((1, n_ctx, d), lambda i, j: (i, 0, 0)),
              pl.BlockSpec((1, n_lat, d), lambda i, j: (i, 0, 0)),
              pl.BlockSpec((1, d), lambda i, j: (0, 0)),
              pl.BlockSpec((1, 3, d), lambda i, j: (i, 0, 0)),
              pl.BlockSpec((3, d), lambda i, j: (0, 0))]
    ut, gt = pl.pallas_call(
        functools.partial(_proj_fm_kernel, n_rope_tiles=rope_tiles),
        grid=(b, n_fm // tn),
        in_specs=common + [pl.BlockSpec((tn, d), lambda i, j: (j, 0)),
                           pl.BlockSpec((tn, 1), lambda i, j: (j, 0)),
                           pl.BlockSpec(wg.shape, lambda i, j: (0, 0)),
                           pl.BlockSpec(bg.shape, lambda i, j: (0, 0)),
                           pl.BlockSpec(cos_t.shape, lambda i, j: (0, 0)),
                           pl.BlockSpec(sin_t.shape, lambda i, j: (0, 0))],
        out_specs=[pl.BlockSpec((1, tn, t), lambda i, j: (i, j, 0)),
                   pl.BlockSpec((1, wg.shape[0], t), lambda i, j: (i, 0, 0))],
        out_shape=[jax.ShapeDtypeStruct((b, n_fm, t), BF16),
                   jax.ShapeDtypeStruct((b, wg.shape[0], t), F32)],
        scratch_shapes=[pltpu.VMEM((t, d), BF16)],
        compiler_params=_params("parallel", "arbitrary"),
        name="proj_fm",
    )(ctx, x, nw, mx, mc, wt, bt, wg, bg, cos_t, sin_t)
    uk = pl.pallas_call(
        functools.partial(_proj_tm_kernel, n_rope_tiles=rope_tiles),
        grid=(b, n_tm // tn),
        in_specs=common + [pl.BlockSpec((d, tn), lambda i, j: (0, j)),
                           pl.BlockSpec((1, tn), lambda i, j: (0, j)),
                           pl.BlockSpec((CONV_W, tn), lambda i, j: (0, j)),
                           pl.BlockSpec((1, tn), lambda i, j: (0, j)),
                           pl.BlockSpec((1, tn), lambda i, j: (0, j)),
                           pl.BlockSpec(cos_k.shape, lambda i, j: (0, 0)),
                           pl.BlockSpec(sin_k.shape, lambda i, j: (0, 0))],
        out_specs=pl.BlockSpec((1, t, tn), lambda i, j: (i, 0, j)),
        out_shape=jax.ShapeDtypeStruct((b, t, n_tm), BF16),
        scratch_shapes=[pltpu.VMEM((t, d), BF16)],
        compiler_params=_params("parallel", "arbitrary"),
        name="proj_tm",
    )(ctx, x, nw, mx, mc, wk, bk, cw, cb, ps, cos_k, sin_k)
    return ut, gt, uk


def _chunk_start(c):
    return pl.multiple_of(c * CHUNK, CHUNK)


def _loop(lo, hi, body, reverse=False, unroll=1):
    n = hi - lo
    if n <= 0:
        return

    def wrapped(i, carry):
        body(hi - 1 - i if reverse else lo + i)
        return carry

    lax.fori_loop(0, n, wrapped, 0, unroll=unroll)


def _unroll(n):
    return max(k for k in (1, 2, 3, 4) if n % k == 0)


def _scan(lo, hi, body, init, reverse=False):
    n = hi - lo
    if n <= 0:
        return init

    def wrapped(i, carry):
        return body(hi - 1 - i if reverse else lo + i, carry)

    return lax.fori_loop(0, n, wrapped, init)


def _head_norm_gate(o, nw_col, z):
    mu = jnp.mean(o, axis=0, keepdims=True)
    oc = o - mu
    var = jnp.mean(oc * oc, axis=0, keepdims=True)
    return oc * lax.rsqrt(var + EPS) * nw_col * _silu(z)


def _ret_kernel(lg_ref, qt_ref, k_ref, vt_ref, zt_ref, nw_ref, y_ref, kv, st, pt_scr, cr_scr,
                *, n_ctx_chunks, n_chunks, out_ctx):
    c_ = CHUNK
    h = pl.program_id(1)
    lg_f = lg_ref[0, h]
    lg_b = lg_ref[1, h]
    lane = lax.broadcasted_iota(jnp.int32, (1, c_), 1).astype(F32)
    zeta_f = jnp.exp(lg_f * (c_ - 1.0 - lane)).astype(BF16)
    zeta_b = jnp.exp(lg_b * lane).astype(BF16)
    xi_f = jnp.exp(lg_f * (lane + 1.0))
    xi_b = jnp.exp(lg_b * (c_ - lane))
    dec_f = jnp.exp(lg_f * jnp.full((1, c_), float(c_), F32))
    dec_b = jnp.exp(lg_b * jnp.full((1, c_), float(c_), F32))
    ki = lax.broadcasted_iota(jnp.int32, (c_, c_), 0)
    qi = lax.broadcasted_iota(jnp.int32, (c_, c_), 1)
    diff = (qi - ki).astype(F32)
    dsum = (jnp.where(diff >= 0, jnp.exp(lg_f * jnp.maximum(diff, 0.0)), 0.0)
            + jnp.where(diff <= 0, jnp.exp(lg_b * jnp.maximum(-diff, 0.0)), 0.0))

    def outer_products(c):
        t0 = _chunk_start(c)
        vt = vt_ref[0, :, pl.ds(t0, c_)]
        lhs = jnp.concatenate([vt * zeta_f, vt * zeta_b], axis=0)
        kv[c] = jnp.dot(lhs, k_ref[0, pl.ds(t0, c_), :], preferred_element_type=F32)

    _loop(0, n_chunks, outer_products, unroll=_unroll(n_chunks))

    def scan_f(c, s):
        st[c, 0:c_, :] = s.astype(BF16)
        return dec_f * s + kv[c, 0:c_, :]

    def scan_b(c, s):
        st[c, c_:2 * c_, :] = s.astype(BF16)
        return dec_b * s + kv[c, c_:2 * c_, :]

    zero = jnp.zeros((c_, c_), F32)
    _scan(0, n_chunks, scan_f, zero)
    sb = _scan(0, n_ctx_chunks, scan_b, zero, reverse=True)
    _scan(n_ctx_chunks, n_chunks, scan_b, sb, reverse=True)

    nw_col = nw_ref[...]

    def scores(c):
        t0 = _chunk_start(c)
        qt = qt_ref[0, :, pl.ds(t0, c_)]
        at = jnp.dot(k_ref[0, pl.ds(t0, c_), :], qt, preferred_element_type=F32)
        crs = jnp.dot(st[c], qt, preferred_element_type=F32)
        pt_scr[c] = (at * dsum).astype(BF16)
        cr_scr[c] = crs[0:c_, :] * xi_f + crs[c_:2 * c_, :] * xi_b

    def outputs(c):
        t0 = _chunk_start(c)
        o = jnp.dot(vt_ref[0, :, pl.ds(t0, c_)], pt_scr[c], preferred_element_type=F32) + cr_scr[c]
        z = zt_ref[0, :, pl.ds(t0, c_)].astype(F32)
        y_ref[0, :, pl.ds(t0, c_)] = _head_norm_gate(o, nw_col, z).astype(BF16)

    lo = 0 if out_ctx else n_ctx_chunks
    if not out_ctx:
        y_ref[0, :, 0:n_ctx_chunks * c_] = jnp.zeros((D_HEAD, n_ctx_chunks * c_), BF16)
    _loop(lo, n_chunks, scores, unroll=_unroll(n_chunks - lo))
    _loop(lo, n_chunks, outputs, unroll=_unroll(n_chunks - lo))


def _ret_mix(lg, ut, uk, nw, n_ctx, out_ctx):
    b, _, t = ut.shape
    n_chunks = t // CHUNK
    hd = D_HEAD

    def fm(name):
        g = FM_GROUPS.index(name)
        return pl.BlockSpec((1, hd, t), lambda i, h: (i, g * N_HEADS + h, 0))

    tm_rk = TM_GROUPS.index("rk")
    kern = functools.partial(_ret_kernel, n_ctx_chunks=n_ctx // CHUNK, n_chunks=n_chunks, out_ctx=out_ctx)
    return pl.pallas_call(
        kern,
        grid=(b, N_HEADS),
        in_specs=[pl.BlockSpec(memory_space=pltpu.SMEM),
                  fm("rq"),
                  pl.BlockSpec((1, t, hd), lambda i, h: (i, 0, tm_rk * N_HEADS + h)),
                  fm("rv"), fm("rz"),
                  pl.BlockSpec((hd, 1), lambda i, h: (h, 0))],
        out_specs=pl.BlockSpec((1, hd, t), lambda i, h: (i, h, 0)),
        out_shape=jax.ShapeDtypeStruct((b, N_HEADS * hd, t), BF16),
        scratch_shapes=[pltpu.VMEM((n_chunks, 2 * hd, hd), F32),
                        pltpu.VMEM((n_chunks, 2 * hd, hd), BF16),
                        pltpu.VMEM((n_chunks, hd, hd), BF16),
                        pltpu.VMEM((n_chunks, hd, hd), F32)],
        compiler_params=_params("parallel", "arbitrary"),
        name="ret_mix",
    )(lg, ut, uk, ut, ut, nw)


def _seg_scan(v, pos, op, fill, reverse):
    t = v.shape[1]
    step = 1
    while step < CHUNK:
        if reverse:
            v = op(v, jnp.where(pos < CHUNK - step, pltpu.roll(v, t - step, 1), fill))
        else:
            v = op(v, jnp.where(pos >= step, pltpu.roll(v, step, 1), fill))
        step *= 2
    return v


def _gate_kernel(g_ref, o_ref, tmp, *, n_ctx_chunks, n_chunks):
    c_ = CHUNK
    nh = N_HEADS
    g = g_ref[0]
    pos = lax.broadcasted_iota(jnp.int32, (nh, g.shape[1]), 1) & (c_ - 1)
    for d in range(2):
        rev = d == 1
        ig = g[2 * d * nh:(2 * d + 1) * nh]
        lf = jax.nn.log_sigmoid(g[(2 * d + 1) * nh:(2 * d + 2) * nh])
        bcum = _seg_scan(lf, pos, jnp.add, 0.0, rev)
        tot = bcum + _seg_scan(lf, pos, jnp.add, 0.0, not rev) - lf
        r = ig - bcum
        cm = _seg_scan(r, pos, jnp.maximum, NEG, rev)
        rmax = jnp.maximum(cm, _seg_scan(r, pos, jnp.maximum, NEG, not rev))
        o_ref[0, 2 * GQ_ROWK + d] = r
        tmp[2 * GT_CM + d] = cm
        tmp[2 * GT_TOT + d] = tot
        tmp[2 * GT_RMAX + d] = rmax
        tmp[2 * GT_BCUM + d] = bcum

    def make_step(d):
        def step(c, m_prev):
            sl = pl.ds(_chunk_start(c), c_)
            gt = tmp[2 * GT_TOT + d, :, sl]
            m_new = jnp.maximum(gt + m_prev, gt + tmp[2 * GT_RMAX + d, :, sl])
            mx = jnp.maximum(tmp[2 * GT_CM + d, :, sl], m_prev)
            o_ref[0, 2 * GQ_COLQ + d, :, sl] = -mx
            o_ref[0, 2 * GQ_SC + d, :, sl] = jnp.exp(m_prev - mx)
            o_ref[0, 2 * GQ_EMM + d, :, sl] = jnp.exp(-(tmp[2 * GT_BCUM + d, :, sl] + mx))
            o_ref[0, 2 * GQ_EW + d, :, sl] = jnp.exp(gt + o_ref[0, 2 * GQ_ROWK + d, :, sl] - m_new)
            o_ref[0, 2 * GQ_DEC + d, :, sl] = jnp.exp(gt + m_prev - m_new)
            return m_new
        return step

    m0 = jnp.zeros((nh, c_), F32)
    _scan(0, n_chunks, make_step(0), m0)
    mb = _scan(0, n_ctx_chunks, make_step(1), m0, reverse=True)
    _scan(n_ctx_chunks, n_chunks, make_step(1), mb, reverse=True)


def _gate_rows(gt, n_ctx):
    b, n_rows, t = gt.shape
    kern = functools.partial(_gate_kernel, n_ctx_chunks=n_ctx // CHUNK, n_chunks=t // CHUNK)
    rows = pl.pallas_call(
        kern,
        grid=(b,),
        in_specs=[pl.BlockSpec((1, n_rows, t), lambda i: (i, 0, 0))],
        out_specs=pl.BlockSpec((1, N_GQ, N_HEADS, t), lambda i: (i, 0, 0, 0)),
        out_shape=jax.ShapeDtypeStruct((b, N_GQ, N_HEADS, t), F32),
        scratch_shapes=[pltpu.VMEM((8, N_HEADS, t), F32)],
        compiler_params=_params("parallel"),
        name="ml_gates",
    )(gt)
    return rows.transpose(0, 2, 1, 3)


def _ml_kernel(q_ref, k_ref, vt_ref, ot_ref, zt_ref, gs_ref, nw_ref, y_ref, kv, st, pt_scr, cr_scr,
               *, n_ctx_chunks, n_chunks, out_ctx):
    c_ = CHUNK

    def grow(q, d, sl):
        return gs_ref[0, 0, 2 * q + d:2 * q + d + 1, sl]

    ones = jnp.ones((ONES_ROWS, c_), BF16)

    def outer_products(c):
        sl = pl.ds(_chunk_start(c), c_)
        vt = vt_ref[0, :, sl]
        ew_f = grow(GQ_EW, 0, sl).astype(BF16)
        ew_b = grow(GQ_EW, 1, sl).astype(BF16)
        lhs = jnp.concatenate([vt * ew_f, ones * ew_f, vt * ew_b, ones * ew_b], axis=0)
        kv[c] = jnp.dot(lhs, k_ref[0, sl, :], preferred_element_type=F32)

    _loop(0, n_chunks, outer_products, unroll=_unroll(n_chunks))

    def make_scan(d):
        def step(c, s):
            sl = pl.ds(_chunk_start(c), c_)
            st[c, d * AUG:(d + 1) * AUG, :] = s.astype(BF16)
            return grow(GQ_DEC, d, sl) * s + kv[c, d * AUG:(d + 1) * AUG, :]
        return step

    zero = jnp.zeros((AUG, c_), F32)
    _scan(0, n_chunks, make_scan(0), zero)
    sb = _scan(0, n_ctx_chunks, make_scan(1), zero, reverse=True)
    _scan(n_ctx_chunks, n_chunks, make_scan(1), sb, reverse=True)

    nw_col = nw_ref[...]
    ki = lax.broadcasted_iota(jnp.int32, (c_, c_), 0)
    qi = lax.broadcasted_iota(jnp.int32, (c_, c_), 1)

    def scores(c):
        sl = pl.ds(_chunk_start(c), c_)
        qc = q_ref[0, sl, :]
        at = lax.dot_general(k_ref[0, sl, :], qc, NT_DIMS, preferred_element_type=F32)
        crs = lax.dot_general(st[c], qc, NT_DIMS, preferred_element_type=F32)
        rowk_f = jnp.broadcast_to(grow(GQ_ROWK, 0, sl), (c_, c_)).T
        rowk_b = jnp.broadcast_to(grow(GQ_ROWK, 1, sl), (c_, c_)).T
        s_f = at * jnp.exp(jnp.where(ki <= qi, rowk_f + grow(GQ_COLQ, 0, sl), NEG))
        s_b = at * jnp.exp(jnp.where(ki >= qi, rowk_b + grow(GQ_COLQ, 1, sl), NEG))
        sc_f = grow(GQ_SC, 0, sl)
        sc_b = grow(GQ_SC, 1, sl)
        den_f = jnp.sum(s_f, axis=0, keepdims=True) + crs[D_HEAD:D_HEAD + 1, :] * sc_f
        den_b = jnp.sum(s_b, axis=0, keepdims=True) + crs[AUG + D_HEAD:AUG + D_HEAD + 1, :] * sc_b
        r_f = 1.0 / jnp.maximum(jnp.abs(den_f), grow(GQ_EMM, 0, sl))
        r_b = 1.0 / jnp.maximum(jnp.abs(den_b), grow(GQ_EMM, 1, sl))
        pt_scr[c] = (s_f * r_f + s_b * r_b).astype(BF16)
        cr_scr[c] = crs[0:D_HEAD, :] * (sc_f * r_f) + crs[AUG:AUG + D_HEAD, :] * (sc_b * r_b)

    def outputs(c):
        sl = pl.ds(_chunk_start(c), c_)
        o = jnp.dot(vt_ref[0, :, sl], pt_scr[c], preferred_element_type=F32) + cr_scr[c]
        o = jax.nn.sigmoid(ot_ref[0, :, sl].astype(F32)) * o
        z = zt_ref[0, :, sl].astype(F32)
        y_ref[0, :, sl] = _head_norm_gate(o, nw_col, z).astype(BF16)

    lo = 0 if out_ctx else n_ctx_chunks
    if not out_ctx:
        y_ref[0, :, 0:n_ctx_chunks * c_] = jnp.zeros((D_HEAD, n_ctx_chunks * c_), BF16)
    _loop(lo, n_chunks, scores, unroll=_unroll(n_chunks - lo))
    _loop(lo, n_chunks, outputs, unroll=_unroll(n_chunks - lo))


def _ml_mix(ut, uk, gs, nw, n_ctx, out_ctx):
    b, _, t = ut.shape
    n_chunks = t // CHUNK
    hd = D_HEAD

    def fm(name):
        g = FM_GROUPS.index(name)
        return pl.BlockSpec((1, hd, t), lambda i, h: (i, g * N_HEADS + h, 0))

    def tm(name):
        g = TM_GROUPS.index(name)
        return pl.BlockSpec((1, t, hd), lambda i, h: (i, 0, g * N_HEADS + h))

    kern = functools.partial(_ml_kernel, n_ctx_chunks=n_ctx // CHUNK, n_chunks=n_chunks, out_ctx=out_ctx)
    return pl.pallas_call(
        kern,
        grid=(b, N_HEADS),
        in_specs=[tm("mq"), tm("mk"), fm("mv"), fm("mo"), fm("mz"),
                  pl.BlockSpec((1, 1, N_GQ, t), lambda i, h: (i, h, 0, 0)),
                  pl.BlockSpec((hd, 1), lambda i, h: (h, 0))],
        out_specs=pl.BlockSpec((1, hd, t), lambda i, h: (i, h, 0)),
        out_shape=jax.ShapeDtypeStruct((b, N_HEADS * hd, t), BF16),
        scratch_shapes=[pltpu.VMEM((n_chunks, 2 * AUG, hd), F32),
                        pltpu.VMEM((n_chunks, 2 * AUG, hd), BF16),
                        pltpu.VMEM((n_chunks, hd, hd), BF16),
                        pltpu.VMEM((n_chunks, hd, hd), F32)],
        compiler_params=_params("parallel", "arbitrary"),
        name="ml_mix",
    )(uk, uk, ut, ut, ut, gs, nw)


def _out_kernel(yr_ref, ym_ref, gr_ref, gm_ref, wro_ref, wmo_ref, wout_ref, x_ref, gate_ref, fnw_ref,
                o_ref, *, final):
    br = jnp.dot(wro_ref[...], yr_ref[0], preferred_element_type=F32)
    bm = jnp.dot(wmo_ref[...], ym_ref[0], preferred_element_type=F32)
    y = (jax.nn.sigmoid(gr_ref[0].astype(F32)) * br + jax.nn.sigmoid(gm_ref[0].astype(F32)) * bm).astype(BF16)
    o = lax.dot_general(y, wout_ref[...], TN_DIMS, preferred_element_type=F32)
    xn = x_ref[0] + gate_ref[0] * o
    if final:
        ms = jnp.mean(xn * xn, axis=-1, keepdims=True)
        xn = xn * lax.rsqrt(ms + EPS) * fnw_ref[...]
    o_ref[0] = xn


def _out(yr, ym, ut, wro_t, wmo_t, wout, xs, gate, fnw, tok0, final):
    b, n, d = xs.shape
    tm = ROW_BLOCK
    assert tok0 % tm == 0 and n % tm == 0
    skip = tok0 // tm
    gr_blk = FM_GROUPS.index("gr") * N_HEADS * D_HEAD // d
    gm_blk = FM_GROUPS.index("gm") * N_HEADS * D_HEAD // d
    kern = functools.partial(_out_kernel, final=final)
    return pl.pallas_call(
        kern,
        grid=(b, n // tm),
        in_specs=[pl.BlockSpec((1, d, tm), lambda i, j: (i, 0, j + skip)),
                  pl.BlockSpec((1, d, tm), lambda i, j: (i, 0, j + skip)),
                  pl.BlockSpec((1, d, tm), lambda i, j: (i, gr_blk, j + skip)),
                  pl.BlockSpec((1, d, tm), lambda i, j: (i, gm_blk, j + skip)),
                  pl.BlockSpec((d, d), lambda i, j: (0, 0)),
                  pl.BlockSpec((d, d), lambda i, j: (0, 0)),
                  pl.BlockSpec((d, d), lambda i, j: (0, 0)),
                  pl.BlockSpec((1, tm, d), lambda i, j: (i, j, 0)),
                  pl.BlockSpec((1, 1, d), lambda i, j: (i, 0, 0)),
                  pl.BlockSpec((1, d), lambda i, j: (0, 0))],
        out_specs=pl.BlockSpec((1, tm, d), lambda i, j: (i, j, 0)),
        out_shape=jax.ShapeDtypeStruct((b, n, d), F32),
        compiler_params=_params("parallel", "parallel"),
        name="out_final" if final else "out",
    )(yr, ym, ut, ut, wro_t, wmo_t, wout, xs, gate, fnw)


def _rope_tables(n_lat):
    rows_n = n_lat // GRID_W
    rows = jnp.repeat(jnp.arange(rows_n, dtype=F32), GRID_W)
    cols = jnp.tile(jnp.arange(GRID_W, dtype=F32), rows_n)
    nf = D_HEAD // 4
    freqs = ROPE_BASE ** (-jnp.arange(nf, dtype=F32) / nf)
    ang = jnp.concatenate([rows[:, None] * freqs, cols[:, None] * freqs], axis=-1)
    cos, sin = jnp.cos(ang), jnp.sin(ang)
    cos_k = jnp.concatenate([cos, cos], axis=-1)
    sin_k = jnp.concatenate([-sin, sin], axis=-1)
    return cos_k.T, sin_k.T, cos_k, sin_k


def _split_weights(w_in, b_in, conv_w, conv_b):
    hd = N_HEADS * D_HEAD
    d = w_in.shape[0]
    off = {name: i * hd for i, name in enumerate(("rq", "rk", "rv", "rz", "mq", "mk", "mv", "mo", "mz"))}
    g_off = 9 * hd
    n_g = 4 * N_HEADS
    off["gr"] = g_off + n_g
    off["gm"] = g_off + n_g + d
    widths = dict.fromkeys(off, hd)
    widths["gr"] = widths["gm"] = d

    def cols(a, name):
        return a[..., off[name]:off[name] + widths[name]]

    wt = jnp.concatenate([cols(w_in, n) for n in FM_GROUPS], axis=-1).T.astype(BF16)
    bt = jnp.concatenate([cols(b_in, n) for n in FM_GROUPS], axis=-1)[:, None]
    wk = jnp.concatenate([cols(w_in, n) for n in TM_GROUPS], axis=-1).astype(BF16)
    bk = jnp.concatenate([cols(b_in, n) for n in TM_GROUPS], axis=-1)[None, :]
    wg = w_in[:, g_off:g_off + n_g].T.astype(BF16)
    bg = b_in[g_off:g_off + n_g][:, None]
    conv_cols = {"mq": slice(0, hd), "mk": slice(hd, 2 * hd)}
    cw = jnp.concatenate([conv_w[:, conv_cols[n]] if n in conv_cols else jnp.zeros((CONV_W, hd), F32)
                          for n in TM_GROUPS], axis=-1)
    cb = jnp.concatenate([conv_b[conv_cols[n]] if n in conv_cols else jnp.zeros((hd,), F32)
                          for n in TM_GROUPS])[None, :]
    ps = jnp.concatenate([jnp.full((hd,), D_HEAD ** -0.5 if n == "mq" else 1.0, F32) for n in TM_GROUPS])[None, :]
    return wt, bt, wg, bg, wk, bk, cw, cb, ps


def kernel(x, c, ctx, c_ctx, norm_w, w_ada, b_ada, w_in, b_in, conv_w, conv_b, ret_log_gamma, ret_norm_w,
           ml_norm_w, w_ret_o, w_ml_o, w_out, final_norm_w):
    b, n_lat, d = x.shape
    n_ctx = ctx.shape[1]
    depth = w_in.shape[0]
    hd = N_HEADS * D_HEAD
    assert d == hd and n_lat % ROW_BLOCK == 0 and n_ctx % ROW_BLOCK == 0 and n_lat % GRID_W == 0
    assert TM_GROUPS[0] == "rk" and FM_GROUPS[0] == "rq"

    rows = -(-(b + 1) // SUBLANES) * SUBLANES
    cc = jnp.zeros((rows, d), F32).at[:b].set(c).at[b].set(c_ctx)
    mods = _adaln(cc, w_ada, b_ada)
    tables = _rope_tables(n_lat)

    for l in range(depth):
        final = l == depth - 1
        mx = mods[l, :b].reshape(b, 3, d)
        mc = mods[l, b].reshape(3, d)
        w = _split_weights(w_in[l], b_in[l], conv_w[l], conv_b[l])
        ut, gt, uk = _proj(ctx, x, norm_w[l][None, :], mx, mc, w, tables)
        yr = _ret_mix(ret_log_gamma[l], ut, uk, ret_norm_w[l][:, None], n_ctx, not final)
        gs = _gate_rows(gt, n_ctx)
        ym = _ml_mix(ut, uk, gs, ml_norm_w[l][:, None], n_ctx, not final)
        wo = (w_ret_o[l].T.astype(BF16), w_ml_o[l].T.astype(BF16), w_out[l].astype(BF16))
        fnw = final_norm_w[None, :]
        if not final:
            ctx = _out(yr, ym, ut, *wo, ctx, jnp.broadcast_to(mc[2], (b, 1, d)), fnw, 0, False)
        x = _out(yr, ym, ut, *wo, x, mx[:, 2:3], fnw, n_ctx, final)
    return x
```

```python
import functools

import jax
import jax.numpy as jnp
from jax import lax
from jax.experimental import pallas as pl
from jax.experimental.pallas import tpu as pltpu

F32 = jnp.float32
BF16 = jnp.bfloat16

N_HEADS = 8
D_HEAD = 128
CHUNK = 128
CONV_W = 5
GRID_W = 64
ROPE_BASE = 10000.0
EPS = 1e-6
NEG = -1e30
SUBLANES = 8
ONES_ROWS = 16
AUG = D_HEAD + ONES_ROWS
ROW_BLOCK = 256
HEADS_PER_STEP = 2

VMEM_LIMIT_BYTES = 56 * 1024 * 1024

NT_DIMS = (((1,), (1,)), ((), ()))
TN_DIMS = (((0,), (0,)), ((), ()))

FM_GROUPS = ("rq", "rv", "rz", "mv", "mo", "mz", "gr", "gm")
TM_GROUPS = ("rk", "mk", "mq")

GQ_ROWK, GQ_COLQ, GQ_SC, GQ_EMM, GQ_EW, GQ_DEC = range(6)
N_GQ = 12
GT_CM, GT_TOT, GT_RMAX, GT_BCUM = range(4)


def _silu(v):
    return v * jax.nn.sigmoid(v)


def _params(*sem):
    return pltpu.CompilerParams(dimension_semantics=sem, vmem_limit_bytes=VMEM_LIMIT_BYTES)


def _adaln_kernel(c_ref, w_ref, b_ref, o_ref):
    s = _silu(c_ref[...])
    o_ref[0] = jnp.dot(s, w_ref[0], preferred_element_type=F32,
                       precision=lax.Precision.HIGHEST) + b_ref[0]


def _adaln(cc, w_ada, b_ada):
    depth, d, d3 = w_ada.shape
    rows = cc.shape[0]
    tn = 512
    return pl.pallas_call(
        _adaln_kernel,
        grid=(depth, d3 // tn),
        in_specs=[pl.BlockSpec((rows, d), lambda l, j: (0, 0)),
                  pl.BlockSpec((1, d, tn), lambda l, j: (l, 0, j)),
                  pl.BlockSpec((1, 1, tn), lambda l, j: (l, 0, j))],
        out_specs=pl.BlockSpec((1, rows, tn), lambda l, j: (l, 0, j)),
        out_shape=jax.ShapeDtypeStruct((depth, rows, d3), F32),
        compiler_params=_params("parallel", "parallel"),
        name="adaln",
    )(cc, w_ada, b_ada.reshape(depth, 1, d3))


def _norm_mod_to_scratch(ctx_ref, x_ref, nw_ref, mx_ref, mc_ref, h_scr):
    rows = ROW_BLOCK
    n_ctx = ctx_ref.shape[1]

    def block(src_ref, shift, scale, src_row, dst_row):
        xv = src_ref[0, pl.ds(src_row, rows), :]
        ms = jnp.mean(xv * xv, axis=-1, keepdims=True)
        y = xv * lax.rsqrt(ms + EPS) * nw_ref[...]
        h_scr[pl.ds(dst_row, rows), :] = (y * (1.0 + scale) + shift).astype(BF16)

    def ctx_body(i, carry):
        r0 = pl.multiple_of(i * rows, rows)
        block(ctx_ref, mc_ref[0:1, :], mc_ref[1:2, :], r0, r0)
        return carry

    def lat_body(i, carry):
        r0 = pl.multiple_of(i * rows, rows)
        block(x_ref, mx_ref[0, 0:1, :], mx_ref[0, 1:2, :], r0, pl.multiple_of(r0 + n_ctx, rows))
        return carry

    lax.fori_loop(0, n_ctx // rows, ctx_body, 0)
    lax.fori_loop(0, x_ref.shape[1] // rows, lat_body, 0)


def _proj_kernel(ctx_ref, x_ref, nw_ref, mx_ref, mc_ref, wt_ref, bt_ref, wg_ref, bg_ref, wk_ref, bk_ref,
                 cw_ref, cb_ref, ps_ref, cos_t_ref, sin_t_ref, cos_k_ref, sin_k_ref,
                 ut_ref, g_ref, uk_ref, h_scr, *, n_rope_tiles, n_tm_tiles):
    j = pl.program_id(1)
    n_ctx = ctx_ref.shape[1]
    tn = wt_ref.shape[0]
    t = h_scr.shape[0]
    rb = ROW_BLOCK
    half = CONV_W // 2

    @pl.when(j == 0)
    def _():
        _norm_mod_to_scratch(ctx_ref, x_ref, nw_ref, mx_ref, mc_ref, h_scr)
        g = lax.dot_general(wg_ref[...], h_scr[...], NT_DIMS, preferred_element_type=F32)
        g_ref[0] = g + bg_ref[...]

    def fm_rope():
        scale = D_HEAD ** -0.5
        for r0 in range(0, t, rb):
            acc = lax.dot_general(wt_ref[...], h_scr[r0:r0 + rb, :], NT_DIMS,
                                  preferred_element_type=F32) + bt_ref[...]
            if r0 >= n_ctx:
                l0 = r0 - n_ctx
                cos, sin = cos_t_ref[:, l0:l0 + rb], sin_t_ref[:, l0:l0 + rb]
                heads = [acc[hh:hh + D_HEAD] for hh in range(0, tn, D_HEAD)]
                acc = jnp.concatenate([q * cos + pltpu.roll(q, D_HEAD // 2, 0) * sin for q in heads], axis=0)
            ut_ref[0, :, r0:r0 + rb] = (acc * scale).astype(BF16)

    def fm_plain():
        acc = lax.dot_general(wt_ref[...], h_scr[...], NT_DIMS, preferred_element_type=F32)
        ut_ref[0] = (acc + bt_ref[...]).astype(BF16)

    def fm_block(r0):
        acc = lax.dot_general(wt_ref[...], h_scr[r0:r0 + rb, :], NT_DIMS, preferred_element_type=F32)
        ut_ref[0, :, r0:r0 + rb] = (acc + bt_ref[...]).astype(BF16)

    def project(r0):
        return jnp.dot(h_scr[r0:r0 + rb, :], wk_ref[...], preferred_element_type=F32) + bk_ref[...]

    def tm_rope():
        for r0 in range(0, t, rb):
            acc = project(r0)
            if r0 >= n_ctx:
                l0 = r0 - n_ctx
                cos, sin = cos_k_ref[l0:l0 + rb, :], sin_k_ref[l0:l0 + rb, :]
                heads = [acc[:, hh:hh + D_HEAD] for hh in range(0, tn, D_HEAD)]
                acc = jnp.concatenate([k * cos + pltpu.roll(k, D_HEAD // 2, 1) * sin for k in heads], axis=1)
            uk_ref[0, r0:r0 + rb, :] = acc.astype(BF16)

    def tm_conv(with_fm):
        zeros = jnp.zeros((SUBLANES, tn), F32)
        seg_edges = (0, n_ctx, t)
        raw = {}

        def conv_block(r0):
            top = zeros if r0 in seg_edges else raw[r0 - rb][rb - SUBLANES:]
            bot = zeros if r0 + rb in seg_edges else raw[r0 + rb][:SUBLANES]
            cur = raw[r0]
            ext = jnp.concatenate([top, cur, bot], axis=0)
            n_ext = rb + 2 * SUBLANES
            acc = cur * cw_ref[half:half + 1, :] + cb_ref[...]
            for jj in range(CONV_W):
                s = jj - half
                if s != 0:
                    shifted = pltpu.roll(ext, (-s) % n_ext, 0)[SUBLANES:SUBLANES + rb]
                    acc = acc + shifted * cw_ref[jj:jj + 1, :]
            uk_ref[0, r0:r0 + rb, :] = (_silu(acc) * ps_ref[...]).astype(BF16)

        for r0 in range(0, t, rb):
            raw[r0] = project(r0)
            if with_fm:
                fm_block(r0)
            if r0 > 0:
                conv_block(r0 - rb)
        conv_block(t - rb)

    @pl.when(j < n_rope_tiles)
    def _():
        fm_rope()
        tm_rope()

    @pl.when((j >= n_rope_tiles) & (j < n_tm_tiles))
    def _():
        tm_conv(True)

    @pl.when(j >= n_tm_tiles)
    def _():
        fm_plain()


def _proj(ctx, x, nw, mx, mc, w, tables):
    b, n_lat, d = x.shape
    n_ctx = ctx.shape[1]
    t = n_ctx + n_lat
    wt, bt, wg, bg, wk, bk, cw, cb, ps = w
    n_fm = wt.shape[0]
    n_tm = wk.shape[1]
    tn = 512
    rope_tiles = N_HEADS * D_HEAD // tn
    tm_tiles = n_tm // tn
    assert tm_tiles <= n_fm // tn
    once = pl.Buffered(1)

    def tm_blk(j):
        return jnp.minimum(j, tm_tiles - 1)

    def const(shape):
        return pl.BlockSpec(shape, lambda i, j: (0,) * len(shape), pipeline_mode=once)

    return pl.pallas_call(
        functools.partial(_proj_kernel, n_rope_tiles=rope_tiles, n_tm_tiles=tm_tiles),
        grid=(b, n_fm // tn),
        in_specs=[pl.BlockSpec((1, n_ctx, d), lambda i, j: (i, 0, 0), pipeline_mode=once),
                  pl.BlockSpec((1, n_lat, d), lambda i, j: (i, 0, 0), pipeline_mode=once),
                  const((1, d)),
                  pl.BlockSpec((1, 3, d), lambda i, j: (i, 0, 0)),
                  const((3, d)),
                  pl.BlockSpec((tn, d), lambda i, j: (j, 0)),
                  pl.BlockSpec((tn, 1), lambda i, j: (j, 0)),
                  const(wg.shape), const(bg.shape),
                  pl.BlockSpec((d, tn), lambda i, j: (0, tm_blk(j))),
                  pl.BlockSpec((1, tn), lambda i, j: (0, tm_blk(j))),
                  pl.BlockSpec((CONV_W, tn), lambda i, j: (0, tm_blk(j))),
                  pl.BlockSpec((1, tn), lambda i, j: (0, tm_blk(j))),
                  pl.BlockSpec((1, tn), lambda i, j: (0, tm_blk(j))),
                  const(tables[0].shape), const(tables[1].shape), const(tables[2].shape), const(tables[3].shape)],
        out_specs=[pl.BlockSpec((1, tn, t), lambda i, j: (i, j, 0)),
                   pl.BlockSpec((1, wg.shape[0], t), lambda i, j: (i, 0, 0)),
                   pl.BlockSpec((1, t, tn), lambda i, j: (i, 0, tm_blk(j)))],
        out_shape=[jax.ShapeDtypeStruct((b, n_fm, t), BF16),
                   jax.ShapeDtypeStruct((b, wg.shape[0], t), F32),
                   jax.ShapeDtypeStruct((b, t, n_tm), BF16)],
        scratch_shapes=[pltpu.VMEM((t, d), BF16)],
        compiler_params=_params("parallel", "arbitrary"),
        name="proj",
    )(ctx, x, nw, mx, mc, wt, bt, wg, bg, wk, bk, cw, cb, ps, *tables)


def _chunk_start(c):
    return pl.multiple_of(c * CHUNK, CHUNK)


def _loop(lo, hi, body, reverse=False, unroll=1):
    n = hi - lo
    if n <= 0:
        return

    def wrapped(i, carry):
        body(hi - 1 - i if reverse else lo + i)
        return carry

    lax.fori_loop(0, n, wrapped, 0, unroll=unroll)


def _unroll(n):
    return max(k for k in (1, 2, 3, 4, 6, 8, 9) if n % k == 0)


def _scan(lo, hi, body, init, reverse=False):
    n = hi - lo
    if n <= 0:
        return init

    def wrapped(i, carry):
        return body(hi - 1 - i if reverse else lo + i, carry)

    return lax.fori_loop(0, n, wrapped, init)


def _head_norm_gate(o, nw_col, z):
    mu = jnp.mean(o, axis=0, keepdims=True)
    oc = o - mu
    var = jnp.mean(oc * oc, axis=0, keepdims=True)
    return oc * lax.rsqrt(var + EPS) * nw_col * _silu(z)


def _ret_kernel(lg_ref, qt_ref, k_ref, vt_ref, zt_ref, nw_ref, y_ref, kv, st, pt_scr, cr_scr,
                *, n_ctx_chunks, n_chunks, out_ctx, heads_per_step):
    for hh in range(heads_per_step):
        _ret_head(lg_ref, qt_ref, k_ref, vt_ref, zt_ref, nw_ref, y_ref, kv, st, pt_scr, cr_scr,
                  pl.program_id(1) * heads_per_step + hh, hh * D_HEAD, n_ctx_chunks, n_chunks, out_ctx)


def _ret_head(lg_ref, qt_ref, k_ref, vt_ref, zt_ref, nw_ref, y_ref, kv, st, pt_scr, cr_scr,
              h, r0, n_ctx_chunks, n_chunks, out_ctx):
    c_ = CHUNK
    hs = slice(r0, r0 + D_HEAD)
    lg_f = lg_ref[0, h]
    lg_b = lg_ref[1, h]
    lane = lax.broadcasted_iota(jnp.int32, (1, c_), 1).astype(F32)
    zeta_f = jnp.exp(lg_f * (c_ - 1.0 - lane)).astype(BF16)
    zeta_b = jnp.exp(lg_b * lane).astype(BF16)
    xi_f = jnp.exp(lg_f * (lane + 1.0))
    xi_b = jnp.exp(lg_b * (c_ - lane))
    dec_f = jnp.exp(lg_f * jnp.full((1, c_), float(c_), F32))
    dec_b = jnp.exp(lg_b * jnp.full((1, c_), float(c_), F32))
    ki = lax.broadcasted_iota(jnp.int32, (c_, c_), 0)
    qi = lax.broadcasted_iota(jnp.int32, (c_, c_), 1)
    diff = (qi - ki).astype(F32)
    dsum = (jnp.where(diff >= 0, jnp.exp(lg_f * jnp.maximum(diff, 0.0)), 0.0)
            + jnp.where(diff <= 0, jnp.exp(lg_b * jnp.maximum(-diff, 0.0)), 0.0))

    def outer_products(c):
        t0 = _chunk_start(c)
        vt = vt_ref[0, hs, pl.ds(t0, c_)]
        lhs = jnp.concatenate([vt * zeta_f, vt * zeta_b], axis=0)
        kv[c] = jnp.dot(lhs, k_ref[0, pl.ds(t0, c_), hs], preferred_element_type=F32)

    _loop(0, n_chunks, outer_products, unroll=_unroll(n_chunks))

    def scan_f(c, s):
        st[c, 0:c_, :] = s.astype(BF16)
        return dec_f * s + kv[c, 0:c_, :]

    def scan_b(c, s):
        st[c, c_:2 * c_, :] = s.astype(BF16)
        return dec_b * s + kv[c, c_:2 * c_, :]

    zero = jnp.zeros((c_, c_), F32)
    _scan(0, n_chunks, scan_f, zero)
    sb = _scan(0, n_ctx_chunks, scan_b, zero, reverse=True)
    _scan(n_ctx_chunks, n_chunks, scan_b, sb, reverse=True)

    nw_col = nw_ref[hs, :]

    def scores(c):
        t0 = _chunk_start(c)
        qt = qt_ref[0, hs, pl.ds(t0, c_)]
        at = jnp.dot(k_ref[0, pl.ds(t0, c_), hs], qt, preferred_element_type=F32)
        crs = jnp.dot(st[c], qt, preferred_element_type=F32)
        pt_scr[c] = (at * dsum).astype(BF16)
        cr_scr[c] = crs[0:c_, :] * xi_f + crs[c_:2 * c_, :] * xi_b

    def outputs(c):
        t0 = _chunk_start(c)
        o = jnp.dot(vt_ref[0, hs, pl.ds(t0, c_)], pt_scr[c], preferred_element_type=F32) + cr_scr[c]
        z = zt_ref[0, hs, pl.ds(t0, c_)].astype(F32)
        y_ref[0, hs, pl.ds(t0, c_)] = _head_norm_gate(o, nw_col, z).astype(BF16)

    lo = 0 if out_ctx else n_ctx_chunks
    if not out_ctx:
        y_ref[0, hs, 0:n_ctx_chunks * c_] = jnp.zeros((D_HEAD, n_ctx_chunks * c_), BF16)
    _loop(lo, n_chunks, scores, unroll=_unroll(n_chunks - lo))
    _loop(lo, n_chunks, outputs, unroll=_unroll(n_chunks - lo))


def _ret_mix(lg, ut, uk, nw, n_ctx, out_ctx):
    b, _, t = ut.shape
    n_chunks = t // CHUNK
    hd = D_HEAD
    hps = HEADS_PER_STEP
    blk = N_HEADS // hps

    def fm(name):
        g = FM_GROUPS.index(name)
        return pl.BlockSpec((1, hps * hd, t), lambda i, h: (i, g * blk + h, 0))

    tm_rk = TM_GROUPS.index("rk")
    kern = functools.partial(_ret_kernel, n_ctx_chunks=n_ctx // CHUNK, n_chunks=n_chunks, out_ctx=out_ctx,
                             heads_per_step=hps)
    return pl.pallas_call(
        kern,
        grid=(b, blk),
        in_specs=[pl.BlockSpec(memory_space=pltpu.SMEM),
                  fm("rq"),
                  pl.BlockSpec((1, t, hps * hd), lambda i, h: (i, 0, tm_rk * blk + h)),
                  fm("rv"), fm("rz"),
                  pl.BlockSpec((hps * hd, 1), lambda i, h: (h, 0))],
        out_specs=pl.BlockSpec((1, hps * hd, t), lambda i, h: (i, h, 0)),
        out_shape=jax.ShapeDtypeStruct((b, N_HEADS * hd, t), BF16),
        scratch_shapes=[pltpu.VMEM((n_chunks, 2 * hd, hd), F32),
                        pltpu.VMEM((n_chunks, 2 * hd, hd), BF16),
                        pltpu.VMEM((n_chunks, hd, hd), BF16),
                        pltpu.VMEM((n_chunks, hd, hd), F32)],
        compiler_params=_params("parallel", "arbitrary"),
        name="ret_mix",
    )(lg, ut, uk, ut, ut, nw)


def _seg_scan(v, pos, op, fill, reverse):
    t = v.shape[1]
    step = 1
    while step < CHUNK:
        if reverse:
            v = op(v, jnp.where(pos < CHUNK - step, pltpu.roll(v, t - step, 1), fill))
        else:
            v = op(v, jnp.where(pos >= step, pltpu.roll(v, step, 1), fill))
        step *= 2
    return v


def _gate_kernel(g_ref, o_ref, tmp, *, n_ctx_chunks, n_chunks):
    c_ = CHUNK
    nh = N_HEADS
    g = g_ref[0]
    pos = lax.broadcasted_iota(jnp.int32, (nh, g.shape[1]), 1) & (c_ - 1)
    for d in range(2):
        rev = d == 1
        ig = g[2 * d * nh:(2 * d + 1) * nh]
        lf = jax.nn.log_sigmoid(g[(2 * d + 1) * nh:(2 * d + 2) * nh])
        bcum = _seg_scan(lf, pos, jnp.add, 0.0, rev)
        tot = bcum + _seg_scan(lf, pos, jnp.add, 0.0, not rev) - lf
        r = ig - bcum
        cm = _seg_scan(r, pos, jnp.maximum, NEG, rev)
        rmax = jnp.maximum(cm, _seg_scan(r, pos, jnp.maximum, NEG, not rev))
        o_ref[0, 2 * GQ_ROWK + d] = r
        tmp[2 * GT_CM + d] = cm
        tmp[2 * GT_TOT + d] = tot
        tmp[2 * GT_RMAX + d] = rmax
        tmp[2 * GT_BCUM + d] = bcum

    def make_step(d):
        def step(c, m_prev):
            sl = pl.ds(_chunk_start(c), c_)
            gt = tmp[2 * GT_TOT + d, :, sl]
            m_new = jnp.maximum(gt + m_prev, gt + tmp[2 * GT_RMAX + d, :, sl])
            mx = jnp.maximum(tmp[2 * GT_CM + d, :, sl], m_prev)
            o_ref[0, 2 * GQ_COLQ + d, :, sl] = -mx
            o_ref[0, 2 * GQ_SC + d, :, sl] = jnp.exp(m_prev - mx)
            o_ref[0, 2 * GQ_EMM + d, :, sl] = jnp.exp(-(tmp[2 * GT_BCUM + d, :, sl] + mx))
            o_ref[0, 2 * GQ_EW + d, :, sl] = jnp.exp(gt + o_ref[0, 2 * GQ_ROWK + d, :, sl] - m_new)
            o_ref[0, 2 * GQ_DEC + d, :, sl] = jnp.exp(gt + m_prev - m_new)
            return m_new
        return step

    m0 = jnp.zeros((nh, c_), F32)
    _scan(0, n_chunks, make_step(0), m0)
    mb = _scan(0, n_ctx_chunks, make_step(1), m0, reverse=True)
    _scan(n_ctx_chunks, n_chunks, make_step(1), mb, reverse=True)


def _gate_rows(gt, n_ctx):
    b, n_rows, t = gt.shape
    kern = functools.partial(_gate_kernel, n_ctx_chunks=n_ctx // CHUNK, n_chunks=t // CHUNK)
    rows = pl.pallas_call(
        kern,
        grid=(b,),
        in_specs=[pl.BlockSpec((1, n_rows, t), lambda i: (i, 0, 0))],
        out_specs=pl.BlockSpec((1, N_GQ, N_HEADS, t), lambda i: (i, 0, 0, 0)),
        out_shape=jax.ShapeDtypeStruct((b, N_GQ, N_HEADS, t), F32),
        scratch_shapes=[pltpu.VMEM((8, N_HEADS, t), F32)],
        compiler_params=_params("parallel"),
        name="ml_gates",
    )(gt)
    return rows.transpose(0, 2, 1, 3)


def _ml_kernel(q_ref, k_ref, vt_ref, ot_ref, zt_ref, gs_ref, nw_ref, y_ref, kv, st, pt_scr, cr_scr,
               *, n_ctx_chunks, n_chunks, out_ctx, heads_per_step):
    for hh in range(heads_per_step):
        _ml_head(q_ref, k_ref, vt_ref, ot_ref, zt_ref, gs_ref, nw_ref, y_ref, kv, st, pt_scr, cr_scr,
                 hh, n_ctx_chunks, n_chunks, out_ctx)


def _ml_head(q_ref, k_ref, vt_ref, ot_ref, zt_ref, gs_ref, nw_ref, y_ref, kv, st, pt_scr, cr_scr,
             hh, n_ctx_chunks, n_chunks, out_ctx):
    c_ = CHUNK
    hs = slice(hh * D_HEAD, (hh + 1) * D_HEAD)

    def grow(q, d, sl):
        return gs_ref[0, hh, 2 * q + d:2 * q + d + 1, sl]

    ones = jnp.ones((ONES_ROWS, c_), BF16)

    def outer_products(c):
        sl = pl.ds(_chunk_start(c), c_)
        vt = vt_ref[0, hs, sl]
        ew_f = grow(GQ_EW, 0, sl).astype(BF16)
        ew_b = grow(GQ_EW, 1, sl).astype(BF16)
        lhs = jnp.concatenate([vt * ew_f, ones * ew_f, vt * ew_b, ones * ew_b], axis=0)
        kv[c] = jnp.dot(lhs, k_ref[0, sl, hs], preferred_element_type=F32)

    _loop(0, n_chunks, outer_products, unroll=_unroll(n_chunks))

    def make_scan(d):
        def step(c, s):
            sl = pl.ds(_chunk_start(c), c_)
            st[c, d * AUG:(d + 1) * AUG, :] = s.astype(BF16)
            return grow(GQ_DEC, d, sl) * s + kv[c, d * AUG:(d + 1) * AUG, :]
        return step

    zero = jnp.zeros((AUG, c_), F32)
    _scan(0, n_chunks, make_scan(0), zero)
    sb = _scan(0, n_ctx_chunks, make_scan(1), zero, reverse=True)
    _scan(n_ctx_chunks, n_chunks, make_scan(1), sb, reverse=True)

    nw_col = nw_ref[hs, :]
    ki = lax.broadcasted_iota(jnp.int32, (c_, c_), 0)
    qi = lax.broadcasted_iota(jnp.int32, (c_, c_), 1)

    def scores(c):
        sl = pl.ds(_chunk_start(c), c_)
        qc = q_ref[0, sl, hs]
        at = lax.dot_general(k_ref[0, sl, hs], qc, NT_DIMS, preferred_element_type=F32)
        crs = lax.dot_general(st[c], qc, NT_DIMS, preferred_element_type=F32)
        rowk_f = jnp.broadcast_to(grow(GQ_ROWK, 0, sl), (c_, c_)).T
        rowk_b = jnp.broadcast_to(grow(GQ_ROWK, 1, sl), (c_, c_)).T
        s_f = at * jnp.exp(jnp.where(ki <= qi, rowk_f + grow(GQ_COLQ, 0, sl), NEG))
        s_b = at * jnp.exp(jnp.where(ki >= qi, rowk_b + grow(GQ_COLQ, 1, sl), NEG))
        sc_f = grow(GQ_SC, 0, sl)
        sc_b = grow(GQ_SC, 1, sl)
        den_f = jnp.sum(s_f, axis=0, keepdims=True) + crs[D_HEAD:D_HEAD + 1, :] * sc_f
        den_b = jnp.sum(s_b, axis=0, keepdims=True) + crs[AUG + D_HEAD:AUG + D_HEAD + 1, :] * sc_b
        r_f = 1.0 / jnp.maximum(jnp.abs(den_f), grow(GQ_EMM, 0, sl))
        r_b = 1.0 / jnp.maximum(jnp.abs(den_b), grow(GQ_EMM, 1, sl))
        pt_scr[c] = (s_f * r_f + s_b * r_b).astype(BF16)
        cr_scr[c] = crs[0:D_HEAD, :] * (sc_f * r_f) + crs[AUG:AUG + D_HEAD, :] * (sc_b * r_b)

    def outputs(c):
        sl = pl.ds(_chunk_start(c), c_)
        o = jnp.dot(vt_ref[0, hs, sl], pt_scr[c], preferred_element_type=F32) + cr_scr[c]
        o = jax.nn.sigmoid(ot_ref[0, hs, sl].astype(F32)) * o
        z = zt_ref[0, hs, sl].astype(F32)
        y_ref[0, hs, sl] = _head_norm_gate(o, nw_col, z).astype(BF16)

    lo = 0 if out_ctx else n_ctx_chunks
    if not out_ctx:
        y_ref[0, hs, 0:n_ctx_chunks * c_] = jnp.zeros((D_HEAD, n_ctx_chunks * c_), BF16)
    _loop(lo, n_chunks, scores, unroll=_unroll(n_chunks - lo))
    _loop(lo, n_chunks, outputs, unroll=_unroll(n_chunks - lo))


def _ml_mix(ut, uk, gs, nw, n_ctx, out_ctx):
    b, _, t = ut.shape
    n_chunks = t // CHUNK
    hd = D_HEAD
    hps = HEADS_PER_STEP
    blk = N_HEADS // hps

    def fm(name):
        g = FM_GROUPS.index(name)
        return pl.BlockSpec((1, hps * hd, t), lambda i, h: (i, g * blk + h, 0))

    def tm(name):
        g = TM_GROUPS.index(name)
        return pl.BlockSpec((1, t, hps * hd), lambda i, h: (i, 0, g * blk + h))

    kern = functools.partial(_ml_kernel, n_ctx_chunks=n_ctx // CHUNK, n_chunks=n_chunks, out_ctx=out_ctx,
                             heads_per_step=hps)
    return pl.pallas_call(
        kern,
        grid=(b, blk),
        in_specs=[tm("mq"), tm("mk"), fm("mv"), fm("mo"), fm("mz"),
                  pl.BlockSpec((1, hps, N_GQ, t), lambda i, h: (i, h, 0, 0)),
                  pl.BlockSpec((hps * hd, 1), lambda i, h: (h, 0))],
        out_specs=pl.BlockSpec((1, hps * hd, t), lambda i, h: (i, h, 0)),
        out_shape=jax.ShapeDtypeStruct((b, N_HEADS * hd, t), BF16),
        scratch_shapes=[pltpu.VMEM((n_chunks, 2 * AUG, hd), F32),
                        pltpu.VMEM((n_chunks, 2 * AUG, hd), BF16),
                        pltpu.VMEM((n_chunks, hd, hd), BF16),
                        pltpu.VMEM((n_chunks, hd, hd), F32)],
        compiler_params=_params("parallel", "arbitrary"),
        name="ml_mix",
    )(uk, uk, ut, ut, ut, gs, nw)


def _out_kernel(yr_ref, ym_ref, gr_ref, gm_ref, wro_ref, wmo_ref, wout_ref, x_ref, gate_ref, fnw_ref,
                o_ref, *, final):
    br = jnp.dot(wro_ref[...], yr_ref[0], preferred_element_type=F32)
    bm = jnp.dot(wmo_ref[...], ym_ref[0], preferred_element_type=F32)
    y = (jax.nn.sigmoid(gr_ref[0].astype(F32)) * br + jax.nn.sigmoid(gm_ref[0].astype(F32)) * bm).astype(BF16)
    o = lax.dot_general(y, wout_ref[...], TN_DIMS, preferred_element_type=F32)
    xn = x_ref[0] + gate_ref[0] * o
    if final:
        ms = jnp.mean(xn * xn, axis=-1, keepdims=True)
        xn = xn * lax.rsqrt(ms + EPS) * fnw_ref[...]
    o_ref[0] = xn


def _out(yr, ym, ut, wro_t, wmo_t, wout, xs, gate, fnw, tok0, final):
    b, n, d = xs.shape
    tm = ROW_BLOCK
    assert tok0 % tm == 0 and n % tm == 0
    skip = tok0 // tm
    gr_blk = FM_GROUPS.index("gr") * N_HEADS * D_HEAD // d
    gm_blk = FM_GROUPS.index("gm") * N_HEADS * D_HEAD // d
    kern = functools.partial(_out_kernel, final=final)
    return pl.pallas_call(
        kern,
        grid=(b, n // tm),
        in_specs=[pl.BlockSpec((1, d, tm), lambda i, j: (i, 0, j + skip)),
                  pl.BlockSpec((1, d, tm), lambda i, j: (i, 0, j + skip)),
                  pl.BlockSpec((1, d, tm), lambda i, j: (i, gr_blk, j + skip)),
                  pl.BlockSpec((1, d, tm), lambda i, j: (i, gm_blk, j + skip)),
                  pl.BlockSpec((d, d), lambda i, j: (0, 0)),
                  pl.BlockSpec((d, d), lambda i, j: (0, 0)),
                  pl.BlockSpec((d, d), lambda i, j: (0, 0)),
                  pl.BlockSpec((1, tm, d), lambda i, j: (i, j, 0)),
                  pl.BlockSpec((1, 1, d), lambda i, j: (i, 0, 0)),
                  pl.BlockSpec((1, d), lambda i, j: (0, 0))],
        out_specs=pl.BlockSpec((1, tm, d), lambda i, j: (i, j, 0)),
        out_shape=jax.ShapeDtypeStruct((b, n, d), F32),
        compiler_params=_params("parallel", "parallel"),
        name="out_final" if final else "out",
    )(yr, ym, ut, ut, wro_t, wmo_t, wout, xs, gate, fnw)


def _rope_tables(n_lat):
    rows_n = n_lat // GRID_W
    rows = jnp.repeat(jnp.arange(rows_n, dtype=F32), GRID_W)
    cols = jnp.tile(jnp.arange(GRID_W, dtype=F32), rows_n)
    nf = D_HEAD // 4
    freqs = ROPE_BASE ** (-jnp.arange(nf, dtype=F32) / nf)
    ang = jnp.concatenate([rows[:, None] * freqs, cols[:, None] * freqs], axis=-1)
    cos, sin = jnp.cos(ang), jnp.sin(ang)
    cos_k = jnp.concatenate([cos, cos], axis=-1)
    sin_k = jnp.concatenate([-sin, sin], axis=-1)
    return cos_k.T, sin_k.T, cos_k, sin_k


def _split_weights(w_in, b_in, conv_w, conv_b):
    hd = N_HEADS * D_HEAD
    d = w_in.shape[0]
    off = {name: i * hd for i, name in enumerate(("rq", "rk", "rv", "rz", "mq", "mk", "mv", "mo", "mz"))}
    g_off = 9 * hd
    n_g = 4 * N_HEADS
    off["gr"] = g_off + n_g
    off["gm"] = g_off + n_g + d
    widths = dict.fromkeys(off, hd)
    widths["gr"] = widths["gm"] = d

    def cols(a, name):
        return a[..., off[name]:off[name] + widths[name]]

    wt = jnp.concatenate([cols(w_in, n) for n in FM_GROUPS], axis=-1).T.astype(BF16)
    bt = jnp.concatenate([cols(b_in, n) for n in FM_GROUPS], axis=-1)[:, None]
    wk = jnp.concatenate([cols(w_in, n) for n in TM_GROUPS], axis=-1).astype(BF16)
    bk = jnp.concatenate([cols(b_in, n) for n in TM_GROUPS], axis=-1)[None, :]
    wg = w_in[:, g_off:g_off + n_g].T.astype(BF16)
    bg = b_in[g_off:g_off + n_g][:, None]
    conv_cols = {"mq": slice(0, hd), "mk": slice(hd, 2 * hd)}
    cw = jnp.concatenate([conv_w[:, conv_cols[n]] if n in conv_cols else jnp.zeros((CONV_W, hd), F32)
                          for n in TM_GROUPS], axis=-1)
    cb = jnp.concatenate([conv_b[conv_cols[n]] if n in conv_cols else jnp.zeros((hd,), F32)
                          for n in TM_GROUPS])[None, :]
    ps = jnp.concatenate([jnp.full((hd,), D_HEAD ** -0.5 if n == "mq" else 1.0, F32) for n in TM_GROUPS])[None, :]
    return wt, bt, wg, bg, wk, bk, cw, cb, ps


def kernel(x, c, ctx, c_ctx, norm_w, w_ada, b_ada, w_in, b_in, conv_w, conv_b, ret_log_gamma, ret_norm_w,
           ml_norm_w, w_ret_o, w_ml_o, w_out, final_norm_w):
    b, n_lat, d = x.shape
    n_ctx = ctx.shape[1]
    depth = w_in.shape[0]
    hd = N_HEADS * D_HEAD
    assert d == hd and n_lat % ROW_BLOCK == 0 and n_ctx % ROW_BLOCK == 0 and n_lat % GRID_W == 0
    assert TM_GROUPS[0] == "rk" and FM_GROUPS[0] == "rq"

    rows = -(-(b + 1) // SUBLANES) * SUBLANES
    cc = jnp.zeros((rows, d), F32).at[:b].set(c).at[b].set(c_ctx)
    mods = _adaln(cc, w_ada, b_ada)
    tables = _rope_tables(n_lat)

    for l in range(depth):
        final = l == depth - 1
        mx = mods[l, :b].reshape(b, 3, d)
        mc = mods[l, b].reshape(3, d)
        w = _split_weights(w_in[l], b_in[l], conv_w[l], conv_b[l])
        ut, gt, uk = _proj(ctx, x, norm_w[l][None, :], mx, mc, w, tables)
        yr = _ret_mix(ret_log_gamma[l], ut, uk, ret_norm_w[l][:, None], n_ctx, not final)
        gs = _gate_rows(gt, n_ctx)
        ym = _ml_mix(ut, uk, gs, ml_norm_w[l][:, None], n_ctx, not final)
        wo = (w_ret_o[l].T.astype(BF16), w_ml_o[l].T.astype(BF16), w_out[l].astype(BF16))
        fnw = final_norm_w[None, :]
        if not final:
            ctx = _out(yr, ym, ut, *wo, ctx, jnp.broadcast_to(mc[2], (b, 1, d)), fnw, 0, False)
        x = _out(yr, ym, ut, *wo, x, mx[:, 2:3], fnw, n_ctx, final)
    return x
```

```python
import functools

import jax
import jax.numpy as jnp
from jax import lax
from jax.experimental import pallas as pl
from jax.experimental.pallas import tpu as pltpu

F32 = jnp.float32
BF16 = jnp.bfloat16

N_HEADS = 8
D_HEAD = 128
CHUNK = 128
CONV_W = 5
GRID_W = 64
ROPE_BASE = 10000.0
EPS = 1e-6
NEG = -1e30
SUBLANES = 8
ONES_ROWS = 16
AUG = D_HEAD + ONES_ROWS
ROW_BLOCK = 256
HEADS_PER_STEP = 2

VMEM_LIMIT_BYTES = 56 * 1024 * 1024

NT_DIMS = (((1,), (1,)), ((), ()))
TN_DIMS = (((0,), (0,)), ((), ()))

FM_GROUPS = ("rq", "rv", "rz", "mv", "mo", "mz", "gr", "gm")
TM_GROUPS = ("rk", "mk", "mq")

GQ_ROWK, GQ_COLQ, GQ_SC, GQ_EMM, GQ_EW, GQ_DEC = range(6)
N_GQ = 12
GT_CM, GT_TOT, GT_RMAX, GT_BCUM = range(4)


def _silu(v):
    return v * jax.nn.sigmoid(v)


def _params(*sem):
    return pltpu.CompilerParams(dimension_semantics=sem, vmem_limit_bytes=VMEM_LIMIT_BYTES)


def _adaln_kernel(c_ref, w_ref, b_ref, o_ref):
    s = _silu(c_ref[...])
    o_ref[0] = jnp.dot(s, w_ref[0], preferred_element_type=F32,
                       precision=lax.Precision.HIGHEST) + b_ref[0]


def _adaln(cc, w_ada, b_ada):
    depth, d, d3 = w_ada.shape
    rows = cc.shape[0]
    tn = 512
    return pl.pallas_call(
        _adaln_kernel,
        grid=(depth, d3 // tn),
        in_specs=[pl.BlockSpec((rows, d), lambda l, j: (0, 0)),
                  pl.BlockSpec((1, d, tn), lambda l, j: (l, 0, j)),
                  pl.BlockSpec((1, 1, tn), lambda l, j: (l, 0, j))],
        out_specs=pl.BlockSpec((1, rows, tn), lambda l, j: (l, 0, j)),
        out_shape=jax.ShapeDtypeStruct((depth, rows, d3), F32),
        compiler_params=_params("parallel", "parallel"),
        name="adaln",
    )(cc, w_ada, b_ada.reshape(depth, 1, d3))


def _norm_mod_to_scratch(ctx_ref, x_ref, nw_ref, mx_ref, mc_ref, h_scr):
    rows = ROW_BLOCK
    n_ctx = ctx_ref.shape[1]

    def block(src_ref, shift, scale, src_row, dst_row):
        xv = src_ref[0, pl.ds(src_row, rows), :]
        ms = jnp.mean(xv * xv, axis=-1, keepdims=True)
        y = xv * lax.rsqrt(ms + EPS) * nw_ref[...]
        h_scr[pl.ds(dst_row, rows), :] = (y * (1.0 + scale) + shift).astype(BF16)

    def ctx_body(i, carry):
        r0 = pl.multiple_of(i * rows, rows)
        block(ctx_ref, mc_ref[0:1, :], mc_ref[1:2, :], r0, r0)
        return carry

    def lat_body(i, carry):
        r0 = pl.multiple_of(i * rows, rows)
        block(x_ref, mx_ref[0, 0:1, :], mx_ref[0, 1:2, :], r0, pl.multiple_of(r0 + n_ctx, rows))
        return carry

    lax.fori_loop(0, n_ctx // rows, ctx_body, 0)
    lax.fori_loop(0, x_ref.shape[1] // rows, lat_body, 0)


def _proj_kernel(ctx_ref, x_ref, nw_ref, mx_ref, mc_ref, wt_ref, bt_ref, wg_ref, bg_ref, wk_ref, bk_ref,
                 cw_ref, cb_ref, ps_ref, cos_t_ref, sin_t_ref, cos_k_ref, sin_k_ref,
                 ut_ref, g_ref, uk_ref, h_scr, *, n_rope_tiles, n_tm_tiles):
    j = pl.program_id(1)
    n_ctx = ctx_ref.shape[1]
    tn = wt_ref.shape[0]
    t = h_scr.shape[0]
    rb = ROW_BLOCK
    half = CONV_W // 2

    @pl.when(j == 0)
    def _():
        _norm_mod_to_scratch(ctx_ref, x_ref, nw_ref, mx_ref, mc_ref, h_scr)
        g = lax.dot_general(wg_ref[...], h_scr[...], NT_DIMS, preferred_element_type=F32)
        g_ref[0] = g + bg_ref[...]

    def fm_rope():
        scale = D_HEAD ** -0.5
        for r0 in range(0, t, rb):
            acc = lax.dot_general(wt_ref[...], h_scr[r0:r0 + rb, :], NT_DIMS,
                                  preferred_element_type=F32) + bt_ref[...]
            if r0 >= n_ctx:
                l0 = r0 - n_ctx
                cos, sin = cos_t_ref[:, l0:l0 + rb], sin_t_ref[:, l0:l0 + rb]
                heads = [acc[hh:hh + D_HEAD] for hh in range(0, tn, D_HEAD)]
                acc = jnp.concatenate([q * cos + pltpu.roll(q, D_HEAD // 2, 0) * sin for q in heads], axis=0)
            ut_ref[0, :, r0:r0 + rb] = (acc * scale).astype(BF16)

    def fm_plain():
        acc = lax.dot_general(wt_ref[...], h_scr[...], NT_DIMS, preferred_element_type=F32)
        ut_ref[0] = (acc + bt_ref[...]).astype(BF16)

    def fm_block(r0):
        acc = lax.dot_general(wt_ref[...], h_scr[r0:r0 + rb, :], NT_DIMS, preferred_element_type=F32)
        ut_ref[0, :, r0:r0 + rb] = (acc + bt_ref[...]).astype(BF16)

    def project(r0):
        return jnp.dot(h_scr[r0:r0 + rb, :], wk_ref[...], preferred_element_type=F32) + bk_ref[...]

    def tm_rope():
        for r0 in range(0, t, rb):
            acc = project(r0)
            if r0 >= n_ctx:
                l0 = r0 - n_ctx
                cos, sin = cos_k_ref[l0:l0 + rb, :], sin_k_ref[l0:l0 + rb, :]
                heads = [acc[:, hh:hh + D_HEAD] for hh in range(0, tn, D_HEAD)]
                acc = jnp.concatenate([k * cos + pltpu.roll(k, D_HEAD // 2, 1) * sin for k in heads], axis=1)
            uk_ref[0, r0:r0 + rb, :] = acc.astype(BF16)

    def tm_conv(with_fm):
        zeros = jnp.zeros((SUBLANES, tn), F32)
        seg_edges = (0, n_ctx, t)
        raw = {}

        def conv_block(r0):
            top = zeros if r0 in seg_edges else raw[r0 - rb][rb - SUBLANES:]
            bot = zeros if r0 + rb in seg_edges else raw[r0 + rb][:SUBLANES]
            cur = raw[r0]
            ext = jnp.concatenate([top, cur, bot], axis=0)
            n_ext = rb + 2 * SUBLANES
            acc = cur * cw_ref[half:half + 1, :] + cb_ref[...]
            for jj in range(CONV_W):
                s = jj - half
                if s != 0:
                    shifted = pltpu.roll(ext, (-s) % n_ext, 0)[SUBLANES:SUBLANES + rb]
                    acc = acc + shifted * cw_ref[jj:jj + 1, :]
            uk_ref[0, r0:r0 + rb, :] = (_silu(acc) * ps_ref[...]).astype(BF16)

        for r0 in range(0, t, rb):
            raw[r0] = project(r0)
            if with_fm:
                fm_block(r0)
            if r0 > 0:
                conv_block(r0 - rb)
        conv_block(t - rb)

    @pl.when(j < n_rope_tiles)
    def _():
        fm_rope()
        tm_rope()

    @pl.when((j >= n_rope_tiles) & (j < n_tm_tiles))
    def _():
        tm_conv(True)

    @pl.when(j >= n_tm_tiles)
    def _():
        fm_plain()


def _proj(ctx, x, nw, mx, mc, w, tables):
    b, n_lat, d = x.shape
    n_ctx = ctx.shape[1]
    t = n_ctx + n_lat
    wt, bt, wg, bg, wk, bk, cw, cb, ps = w
    n_fm = wt.shape[0]
    n_tm = wk.shape[1]
    tn = 512
    rope_tiles = N_HEADS * D_HEAD // tn
    tm_tiles = n_tm // tn
    assert tm_tiles <= n_fm // tn
    once = pl.Buffered(1)

    def tm_blk(j):
        return jnp.minimum(j, tm_tiles - 1)

    def const(shape):
        return pl.BlockSpec(shape, lambda i, j: (0,) * len(shape), pipeline_mode=once)

    return pl.pallas_call(
        functools.partial(_proj_kernel, n_rope_tiles=rope_tiles, n_tm_tiles=tm_tiles),
        grid=(b, n_fm // tn),
        in_specs=[pl.BlockSpec((1, n_ctx, d), lambda i, j: (i, 0, 0)),
                  pl.BlockSpec((1, n_lat, d), lambda i, j: (i, 0, 0)),
                  const((1, d)),
                  pl.BlockSpec((1, 3, d), lambda i, j: (i, 0, 0)),
                  const((3, d)),
                  pl.BlockSpec((tn, d), lambda i, j: (j, 0)),
                  pl.BlockSpec((tn, 1), lambda i, j: (j, 0)),
                  const(wg.shape), const(bg.shape),
                  pl.BlockSpec((d, tn), lambda i, j: (0, tm_blk(j))),
                  pl.BlockSpec((1, tn), lambda i, j: (0, tm_blk(j))),
                  pl.BlockSpec((CONV_W, tn), lambda i, j: (0, tm_blk(j))),
                  pl.BlockSpec((1, tn), lambda i, j: (0, tm_blk(j))),
                  pl.BlockSpec((1, tn), lambda i, j: (0, tm_blk(j))),
                  const(tables[0].shape), const(tables[1].shape), const(tables[2].shape), const(tables[3].shape)],
        out_specs=[pl.BlockSpec((1, tn, t), lambda i, j: (i, j, 0)),
                   pl.BlockSpec((1, wg.shape[0], t), lambda i, j: (i, 0, 0)),
                   pl.BlockSpec((1, t, tn), lambda i, j: (i, 0, tm_blk(j)))],
        out_shape=[jax.ShapeDtypeStruct((b, n_fm, t), BF16),
                   jax.ShapeDtypeStruct((b, wg.shape[0], t), F32),
                   jax.ShapeDtypeStruct((b, t, n_tm), BF16)],
        scratch_shapes=[pltpu.VMEM((t, d), BF16)],
        compiler_params=_params("parallel", "arbitrary"),
        name="proj",
    )(ctx, x, nw, mx, mc, wt, bt, wg, bg, wk, bk, cw, cb, ps, *tables)


def _chunk_start(c):
    return pl.multiple_of(c * CHUNK, CHUNK)


def _loop(lo, hi, body, reverse=False, unroll=1):
    n = hi - lo
    if n <= 0:
        return

    def wrapped(i, carry):
        body(hi - 1 - i if reverse else lo + i)
        return carry

    lax.fori_loop(0, n, wrapped, 0, unroll=unroll)


def _unroll(n):
    return max(k for k in (1, 2, 3, 4, 6, 8, 9) if n % k == 0)


def _scan(lo, hi, body, init, reverse=False):
    n = hi - lo
    if n <= 0:
        return init

    def wrapped(i, carry):
        return body(hi - 1 - i if reverse else lo + i, carry)

    return lax.fori_loop(0, n, wrapped, init)


def _head_norm_gate(o, nw_col, z):
    mu = jnp.mean(o, axis=0, keepdims=True)
    oc = o - mu
    var = jnp.mean(oc * oc, axis=0, keepdims=True)
    return oc * lax.rsqrt(var + EPS) * nw_col * _silu(z)


def _ret_kernel(lg_ref, qt_ref, k_ref, vt_ref, zt_ref, nw_ref, y_ref, kv, st, pt_scr, cr_scr,
                *, n_ctx_chunks, n_chunks, out_ctx, heads_per_step):
    for hh in range(heads_per_step):
        _ret_head(lg_ref, qt_ref, k_ref, vt_ref, zt_ref, nw_ref, y_ref, kv, st, pt_scr, cr_scr,
                  pl.program_id(1) * heads_per_step + hh, hh * D_HEAD, n_ctx_chunks, n_chunks, out_ctx)


def _ret_head(lg_ref, qt_ref, k_ref, vt_ref, zt_ref, nw_ref, y_ref, kv, st, pt_scr, cr_scr,
              h, r0, n_ctx_chunks, n_chunks, out_ctx):
    c_ = CHUNK
    hs = slice(r0, r0 + D_HEAD)
    lg_f = lg_ref[0, h]
    lg_b = lg_ref[1, h]
    lane = lax.broadcasted_iota(jnp.int32, (1, c_), 1).astype(F32)
    zeta_f = jnp.exp(lg_f * (c_ - 1.0 - lane)).astype(BF16)
    zeta_b = jnp.exp(lg_b * lane).astype(BF16)
    xi_f = jnp.exp(lg_f * (lane + 1.0))
    xi_b = jnp.exp(lg_b * (c_ - lane))
    dec_f = jnp.exp(lg_f * jnp.full((1, c_), float(c_), F32))
    dec_b = jnp.exp(lg_b * jnp.full((1, c_), float(c_), F32))
    ki = lax.broadcasted_iota(jnp.int32, (c_, c_), 0)
    qi = lax.broadcasted_iota(jnp.int32, (c_, c_), 1)
    diff = (qi - ki).astype(F32)
    dsum = (jnp.where(diff >= 0, jnp.exp(lg_f * jnp.maximum(diff, 0.0)), 0.0)
            + jnp.where(diff <= 0, jnp.exp(lg_b * jnp.maximum(-diff, 0.0)), 0.0))

    def outer_products(c):
        t0 = _chunk_start(c)
        vt = vt_ref[0, hs, pl.ds(t0, c_)]
        lhs = jnp.concatenate([vt * zeta_f, vt * zeta_b], axis=0)
        kv[c] = jnp.dot(lhs, k_ref[0, pl.ds(t0, c_), hs], preferred_element_type=F32)

    _loop(0, n_chunks, outer_products, unroll=_unroll(n_chunks))

    def scan_f(c, s):
        st[c, 0:c_, :] = s.astype(BF16)
        return dec_f * s + kv[c, 0:c_, :]

    def scan_b(c, s):
        st[c, c_:2 * c_, :] = s.astype(BF16)
        return dec_b * s + kv[c, c_:2 * c_, :]

    zero = jnp.zeros((c_, c_), F32)
    _scan(0, n_chunks, scan_f, zero)
    sb = _scan(0, n_ctx_chunks, scan_b, zero, reverse=True)
    _scan(n_ctx_chunks, n_chunks, scan_b, sb, reverse=True)

    nw_col = nw_ref[hs, :]

    def scores(c):
        t0 = _chunk_start(c)
        qt = qt_ref[0, hs, pl.ds(t0, c_)]
        at = jnp.dot(k_ref[0, pl.ds(t0, c_), hs], qt, preferred_element_type=F32)
        crs = jnp.dot(st[c], qt, preferred_element_type=F32)
        pt_scr[c] = (at * dsum).astype(BF16)
        cr_scr[c] = crs[0:c_, :] * xi_f + crs[c_:2 * c_, :] * xi_b

    def outputs(c):
        t0 = _chunk_start(c)
        o = jnp.dot(vt_ref[0, hs, pl.ds(t0, c_)], pt_scr[c], preferred_element_type=F32) + cr_scr[c]
        z = zt_ref[0, hs, pl.ds(t0, c_)].astype(F32)
        y_ref[0, hs, pl.ds(t0, c_)] = _head_norm_gate(o, nw_col, z).astype(BF16)

    lo = 0 if out_ctx else n_ctx_chunks
    if not out_ctx:
        y_ref[0, hs, 0:n_ctx_chunks * c_] = jnp.zeros((D_HEAD, n_ctx_chunks * c_), BF16)
    _loop(lo, n_chunks, scores, unroll=_unroll(n_chunks - lo))
    _loop(lo, n_chunks, outputs, unroll=_unroll(n_chunks - lo))


def _ret_mix(lg, ut, uk, nw, n_ctx, out_ctx):
    b, _, t = ut.shape
    n_chunks = t // CHUNK
    hd = D_HEAD
    hps = HEADS_PER_STEP
    blk = N_HEADS // hps

    def fm(name):
        g = FM_GROUPS.index(name)
        return pl.BlockSpec((1, hps * hd, t), lambda i, h: (i, g * blk + h, 0))

    tm_rk = TM_GROUPS.index("rk")
    kern = functools.partial(_ret_kernel, n_ctx_chunks=n_ctx // CHUNK, n_chunks=n_chunks, out_ctx=out_ctx,
                             heads_per_step=hps)
    return pl.pallas_call(
        kern,
        grid=(b, blk),
        in_specs=[pl.BlockSpec(memory_space=pltpu.SMEM),
                  fm("rq"),
                  pl.BlockSpec((1, t, hps * hd), lambda i, h: (i, 0, tm_rk * blk + h)),
                  fm("rv"), fm("rz"),
                  pl.BlockSpec((hps * hd, 1), lambda i, h: (h, 0))],
        out_specs=pl.BlockSpec((1, hps * hd, t), lambda i, h: (i, h, 0)),
        out_shape=jax.ShapeDtypeStruct((b, N_HEADS * hd, t), BF16),
        scratch_shapes=[pltpu.VMEM((n_chunks, 2 * hd, hd), F32),
                        pltpu.VMEM((n_chunks, 2 * hd, hd), BF16),
                        pltpu.VMEM((n_chunks, hd, hd), BF16),
                        pltpu.VMEM((n_chunks, hd, hd), F32)],
        compiler_params=_params("parallel", "arbitrary"),
        name="ret_mix",
    )(lg, ut, uk, ut, ut, nw)


def _seg_scan(v, pos, op, fill, reverse):
    t = v.shape[1]
    step = 1
    while step < CHUNK:
        if reverse:
            v = op(v, jnp.where(pos < CHUNK - step, pltpu.roll(v, t - step, 1), fill))
        else:
            v = op(v, jnp.where(pos >= step, pltpu.roll(v, step, 1), fill))
        step *= 2
    return v


def _gate_kernel(g_ref, o_ref, tmp, *, n_ctx_chunks, n_chunks):
    c_ = CHUNK
    nh = N_HEADS
    g = g_ref[0]
    pos = lax.broadcasted_iota(jnp.int32, (nh, g.shape[1]), 1) & (c_ - 1)
    for d in range(2):
        rev = d == 1
        ig = g[2 * d * nh:(2 * d + 1) * nh]
        lf = jax.nn.log_sigmoid(g[(2 * d + 1) * nh:(2 * d + 2) * nh])
        bcum = _seg_scan(lf, pos, jnp.add, 0.0, rev)
        tot = bcum + _seg_scan(lf, pos, jnp.add, 0.0, not rev) - lf
        r = ig - bcum
        cm = _seg_scan(r, pos, jnp.maximum, NEG, rev)
        rmax = jnp.maximum(cm, _seg_scan(r, pos, jnp.maximum, NEG, not rev))
        o_ref[0, 2 * GQ_ROWK + d] = r
        tmp[2 * GT_CM + d] = cm
        tmp[2 * GT_TOT + d] = tot
        tmp[2 * GT_RMAX + d] = rmax
        tmp[2 * GT_BCUM + d] = bcum

    def make_step(d):
        def step(c, m_prev):
            sl = pl.ds(_chunk_start(c), c_)
            gt = tmp[2 * GT_TOT + d, :, sl]
            m_new = jnp.maximum(gt + m_prev, gt + tmp[2 * GT_RMAX + d, :, sl])
            mx = jnp.maximum(tmp[2 * GT_CM + d, :, sl], m_prev)
            o_ref[0, 2 * GQ_COLQ + d, :, sl] = -mx
            o_ref[0, 2 * GQ_SC + d, :, sl] = jnp.exp(m_prev - mx)
            o_ref[0, 2 * GQ_EMM + d, :, sl] = jnp.exp(-(tmp[2 * GT_BCUM + d, :, sl] + mx))
            o_ref[0, 2 * GQ_EW + d, :, sl] = jnp.exp(gt + o_ref[0, 2 * GQ_ROWK + d, :, sl] - m_new)
            o_ref[0, 2 * GQ_DEC + d, :, sl] = jnp.exp(gt + m_prev - m_new)
            return m_new
        return step

    m0 = jnp.zeros((nh, c_), F32)
    _scan(0, n_chunks, make_step(0), m0)
    mb = _scan(0, n_ctx_chunks, make_step(1), m0, reverse=True)
    _scan(n_ctx_chunks, n_chunks, make_step(1), mb, reverse=True)


def _gate_rows(gt, n_ctx):
    b, n_rows, t = gt.shape
    kern = functools.partial(_gate_kernel, n_ctx_chunks=n_ctx // CHUNK, n_chunks=t // CHUNK)
    rows = pl.pallas_call(
        kern,
        grid=(b,),
        in_specs=[pl.BlockSpec((1, n_rows, t), lambda i: (i, 0, 0))],
        out_specs=pl.BlockSpec((1, N_GQ, N_HEADS, t), lambda i: (i, 0, 0, 0)),
        out_shape=jax.ShapeDtypeStruct((b, N_GQ, N_HEADS, t), F32),
        scratch_shapes=[pltpu.VMEM((8, N_HEADS, t), F32)],
        compiler_params=_params("parallel"),
        name="ml_gates",
    )(gt)
    return rows.transpose(0, 2, 1, 3)


def _ml_kernel(q_ref, k_ref, vt_ref, ot_ref, zt_ref, gs_ref, nw_ref, y_ref, kv, st, pt_scr, cr_scr,
               *, n_ctx_chunks, n_chunks, out_ctx, heads_per_step):
    for hh in range(heads_per_step):
        _ml_head(q_ref, k_ref, vt_ref, ot_ref, zt_ref, gs_ref, nw_ref, y_ref, kv, st, pt_scr, cr_scr,
                 hh, n_ctx_chunks, n_chunks, out_ctx)


def _ml_head(q_ref, k_ref, vt_ref, ot_ref, zt_ref, gs_ref, nw_ref, y_ref, kv, st, pt_scr, cr_scr,
             hh, n_ctx_chunks, n_chunks, out_ctx):
    c_ = CHUNK
    hs = slice(hh * D_HEAD, (hh + 1) * D_HEAD)

    def grow(q, d, sl):
        return gs_ref[0, hh, 2 * q + d:2 * q + d + 1, sl]

    ones = jnp.ones((ONES_ROWS, c_), BF16)

    def outer_products(c):
        sl = pl.ds(_chunk_start(c), c_)
        vt = vt_ref[0, hs, sl]
        ew_f = grow(GQ_EW, 0, sl).astype(BF16)
        ew_b = grow(GQ_EW, 1, sl).astype(BF16)
        lhs = jnp.concatenate([vt * ew_f, ones * ew_f, vt * ew_b, ones * ew_b], axis=0)
        kv[c] = jnp.dot(lhs, k_ref[0, sl, hs], preferred_element_type=F32)

    _loop(0, n_chunks, outer_products, unroll=_unroll(n_chunks))

    def make_scan(d):
        def step(c, s):
            sl = pl.ds(_chunk_start(c), c_)
            st[c, d * AUG:(d + 1) * AUG, :] = s.astype(BF16)
            return grow(GQ_DEC, d, sl) * s + kv[c, d * AUG:(d + 1) * AUG, :]
        return step

    zero = jnp.zeros((AUG, c_), F32)
    _scan(0, n_chunks, make_scan(0), zero)
    sb = _scan(0, n_ctx_chunks, make_scan(1), zero, reverse=True)
    _scan(n_ctx_chunks, n_chunks, make_scan(1), sb, reverse=True)

    nw_col = nw_ref[hs, :]
    ki = lax.broadcasted_iota(jnp.int32, (c_, c_), 0)
    qi = lax.broadcasted_iota(jnp.int32, (c_, c_), 1)

    def scores(c):
        sl = pl.ds(_chunk_start(c), c_)
        qc = q_ref[0, sl, hs]
        at = lax.dot_general(k_ref[0, sl, hs], qc, NT_DIMS, preferred_element_type=F32)
        crs = lax.dot_general(st[c], qc, NT_DIMS, preferred_element_type=F32)
        rowk_f = jnp.broadcast_to(grow(GQ_ROWK, 0, sl), (c_, c_)).T
        rowk_b = jnp.broadcast_to(grow(GQ_ROWK, 1, sl), (c_, c_)).T
        s_f = at * jnp.exp(jnp.where(ki <= qi, rowk_f + grow(GQ_COLQ, 0, sl), NEG))
        s_b = at * jnp.exp(jnp.where(ki >= qi, rowk_b + grow(GQ_COLQ, 1, sl), NEG))
        sc_f = grow(GQ_SC, 0, sl)
        sc_b = grow(GQ_SC, 1, sl)
        den_f = jnp.sum(s_f, axis=0, keepdims=True) + crs[D_HEAD:D_HEAD + 1, :] * sc_f
        den_b = jnp.sum(s_b, axis=0, keepdims=True) + crs[AUG + D_HEAD:AUG + D_HEAD + 1, :] * sc_b
        r_f = 1.0 / jnp.maximum(jnp.abs(den_f), grow(GQ_EMM, 0, sl))
        r_b = 1.0 / jnp.maximum(jnp.abs(den_b), grow(GQ_EMM, 1, sl))
        pt_scr[c] = (s_f * r_f + s_b * r_b).astype(BF16)
        cr_scr[c] = crs[0:D_HEAD, :] * (sc_f * r_f) + crs[AUG:AUG + D_HEAD, :] * (sc_b * r_b)

    def outputs(c):
        sl = pl.ds(_chunk_start(c), c_)
        o = jnp.dot(vt_ref[0, hs, sl], pt_scr[c], preferred_element_type=F32) + cr_scr[c]
        o = jax.nn.sigmoid(ot_ref[0, hs, sl].astype(F32)) * o
        z = zt_ref[0, hs, sl].astype(F32)
        y_ref[0, hs, sl] = _head_norm_gate(o, nw_col, z).astype(BF16)

    lo = 0 if out_ctx else n_ctx_chunks
    if not out_ctx:
        y_ref[0, hs, 0:n_ctx_chunks * c_] = jnp.zeros((D_HEAD, n_ctx_chunks * c_), BF16)
    _loop(lo, n_chunks, scores, unroll=_unroll(n_chunks - lo))
    _loop(lo, n_chunks, outputs, unroll=_unroll(n_chunks - lo))


def _ml_mix(ut, uk, gs, nw, n_ctx, out_ctx):
    b, _, t = ut.shape
    n_chunks = t // CHUNK
    hd = D_HEAD
    hps = HEADS_PER_STEP
    blk = N_HEADS // hps

    def fm(name):
        g = FM_GROUPS.index(name)
        return pl.BlockSpec((1, hps * hd, t), lambda i, h: (i, g * blk + h, 0))

    def tm(name):
        g = TM_GROUPS.index(name)
        return pl.BlockSpec((1, t, hps * hd), lambda i, h: (i, 0, g * blk + h))

    kern = functools.partial(_ml_kernel, n_ctx_chunks=n_ctx // CHUNK, n_chunks=n_chunks, out_ctx=out_ctx,
                             heads_per_step=hps)
    return pl.pallas_call(
        kern,
        grid=(b, blk),
        in_specs=[tm("mq"), tm("mk"), fm("mv"), fm("mo"), fm("mz"),
                  pl.BlockSpec((1, hps, N_GQ, t), lambda i, h: (i, h, 0, 0)),
                  pl.BlockSpec((hps * hd, 1), lambda i, h: (h, 0))],
        out_specs=pl.BlockSpec((1, hps * hd, t), lambda i, h: (i, h, 0)),
        out_shape=jax.ShapeDtypeStruct((b, N_HEADS * hd, t), BF16),
        scratch_shapes=[pltpu.VMEM((n_chunks, 2 * AUG, hd), F32),
                        pltpu.VMEM((n_chunks, 2 * AUG, hd), BF16),
                        pltpu.VMEM((n_chunks, hd, hd), BF16),
                        pltpu.VMEM((n_chunks, hd, hd), F32)],
        compiler_params=_params("parallel", "arbitrary"),
        name="ml_mix",
    )(uk, uk, ut, ut, ut, gs, nw)


def _out_kernel(yr_ref, ym_ref, gr_ref, gm_ref, wro_ref, wmo_ref, wout_ref, x_ref, gate_ref, fnw_ref,
                o_ref, *, final):
    br = jnp.dot(wro_ref[...], yr_ref[0], preferred_element_type=F32)
    bm = jnp.dot(wmo_ref[...], ym_ref[0], preferred_element_type=F32)
    y = (jax.nn.sigmoid(gr_ref[0].astype(F32)) * br + jax.nn.sigmoid(gm_ref[0].astype(F32)) * bm).astype(BF16)
    o = lax.dot_general(y, wout_ref[...], TN_DIMS, preferred_element_type=F32)
    xn = x_ref[0] + gate_ref[0] * o
    if final:
        ms = jnp.mean(xn * xn, axis=-1, keepdims=True)
        xn = xn * lax.rsqrt(ms + EPS) * fnw_ref[...]
    o_ref[0] = xn


def _out(yr, ym, ut, wro_t, wmo_t, wout, xs, gate, fnw, tok0, final):
    b, n, d = xs.shape
    tm = 2 * ROW_BLOCK if n % (2 * ROW_BLOCK) == 0 else ROW_BLOCK
    assert n % tm == 0
    gr_row = FM_GROUPS.index("gr") * N_HEADS * D_HEAD
    gm_row = FM_GROUPS.index("gm") * N_HEADS * D_HEAD

    def tok(row0):
        return pl.BlockSpec((pl.Element(1), pl.Element(d), pl.Element(tm)), lambda i, j: (i, pl.multiple_of(j * 0 + row0, 128), pl.multiple_of(tok0 + j * tm, 128)))

    kern = functools.partial(_out_kernel, final=final)
    return pl.pallas_call(
        kern,
        grid=(b, n // tm),
        in_specs=[tok(0), tok(0), tok(gr_row), tok(gm_row),
                  pl.BlockSpec((d, d), lambda i, j: (0, 0)),
                  pl.BlockSpec((d, d), lambda i, j: (0, 0)),
                  pl.BlockSpec((d, d), lambda i, j: (0, 0)),
                  pl.BlockSpec((1, tm, d), lambda i, j: (i, j, 0)),
                  pl.BlockSpec((1, 1, d), lambda i, j: (i, 0, 0)),
                  pl.BlockSpec((1, d), lambda i, j: (0, 0))],
        out_specs=pl.BlockSpec((1, tm, d), lambda i, j: (i, j, 0)),
        out_shape=jax.ShapeDtypeStruct((b, n, d), F32),
        compiler_params=_params("parallel", "parallel"),
        name="out_final" if final else "out",
    )(yr, ym, ut, ut, wro_t, wmo_t, wout, xs, gate, fnw)


def _rope_tables(n_lat):
    rows_n = n_lat // GRID_W
    rows = jnp.repeat(jnp.arange(rows_n, dtype=F32), GRID_W)
    cols = jnp.tile(jnp.arange(GRID_W, dtype=F32), rows_n)
    nf = D_HEAD // 4
    freqs = ROPE_BASE ** (-jnp.arange(nf, dtype=F32) / nf)
    ang = jnp.concatenate([rows[:, None] * freqs, cols[:, None] * freqs], axis=-1)
    cos, sin = jnp.cos(ang), jnp.sin(ang)
    cos_k = jnp.concatenate([cos, cos], axis=-1)
    sin_k = jnp.concatenate([-sin, sin], axis=-1)
    return cos_k.T, sin_k.T, cos_k, sin_k


def _split_weights(w_in, b_in, conv_w, conv_b):
    hd = N_HEADS * D_HEAD
    d = w_in.shape[0]
    off = {name: i * hd for i, name in enumerate(("rq", "rk", "rv", "rz", "mq", "mk", "mv", "mo", "mz"))}
    g_off = 9 * hd
    n_g = 4 * N_HEADS
    off["gr"] = g_off + n_g
    off["gm"] = g_off + n_g + d
    widths = dict.fromkeys(off, hd)
    widths["gr"] = widths["gm"] = d

    def cols(a, name):
        return a[..., off[name]:off[name] + widths[name]]

    wt = jnp.concatenate([cols(w_in, n) for n in FM_GROUPS], axis=-1).T.astype(BF16)
    bt = jnp.concatenate([cols(b_in, n) for n in FM_GROUPS], axis=-1)[:, None]
    wk = jnp.concatenate([cols(w_in, n) for n in TM_GROUPS], axis=-1).astype(BF16)
    bk = jnp.concatenate([cols(b_in, n) for n in TM_GROUPS], axis=-1)[None, :]
    wg = w_in[:, g_off:g_off + n_g].T.astype(BF16)
    bg = b_in[g_off:g_off + n_g][:, None]
    conv_cols = {"mq": slice(0, hd), "mk": slice(hd, 2 * hd)}
    cw = jnp.concatenate([conv_w[:, conv_cols[n]] if n in conv_cols else jnp.zeros((CONV_W, hd), F32)
                          for n in TM_GROUPS], axis=-1)
    cb = jnp.concatenate([conv_b[conv_cols[n]] if n in conv_cols else jnp.zeros((hd,), F32)
                          for n in TM_GROUPS])[None, :]
    ps = jnp.concatenate([jnp.full((hd,), D_HEAD ** -0.5 if n == "mq" else 1.0, F32) for n in TM_GROUPS])[None, :]
    return wt, bt, wg, bg, wk, bk, cw, cb, ps


def kernel(x, c, ctx, c_ctx, norm_w, w_ada, b_ada, w_in, b_in, conv_w, conv_b, ret_log_gamma, ret_norm_w,
           ml_norm_w, w_ret_o, w_ml_o, w_out, final_norm_w):
    b, n_lat, d = x.shape
    n_ctx = ctx.shape[1]
    depth = w_in.shape[0]
    hd = N_HEADS * D_HEAD
    assert d == hd and n_lat % ROW_BLOCK == 0 and n_ctx % ROW_BLOCK == 0 and n_lat % GRID_W == 0
    assert TM_GROUPS[0] == "rk" and FM_GROUPS[0] == "rq"

    rows = -(-(b + 1) // SUBLANES) * SUBLANES
    cc = jnp.zeros((rows, d), F32).at[:b].set(c).at[b].set(c_ctx)
    mods = _adaln(cc, w_ada, b_ada)
    tables = _rope_tables(n_lat)

    for l in range(depth):
        final = l == depth - 1
        mx = mods[l, :b].reshape(b, 3, d)
        mc = mods[l, b].reshape(3, d)
        w = _split_weights(w_in[l], b_in[l], conv_w[l], conv_b[l])
        ut, gt, uk = _proj(ctx, x, norm_w[l][None, :], mx, mc, w, tables)
        yr = _ret_mix(ret_log_gamma[l], ut, uk, ret_norm_w[l][:, None], n_ctx, not final)
        gs = _gate_rows(gt, n_ctx)
        ym = _ml_mix(ut, uk, gs, ml_norm_w[l][:, None], n_ctx, not final)
        wo = (w_ret_o[l].T.astype(BF16), w_ml_o[l].T.astype(BF16), w_out[l].astype(BF16))
        fnw = final_norm_w[None, :]
        if not final:
            ctx = _out(yr, ym, ut, *wo, ctx, jnp.broadcast_to(mc[2], (b, 1, d)), fnw, 0, False)
        x = _out(yr, ym, ut, *wo, x, mx[:, 2:3], fnw, n_ctx, final)
    return x
```

```python
import functools

import jax
import jax.numpy as jnp
from jax import lax
from jax.experimental import pallas as pl
from jax.experimental.pallas import tpu as pltpu

F32 = jnp.float32
BF16 = jnp.bfloat16

N_HEADS = 8
D_HEAD = 128
CHUNK = 128
CONV_W = 5
GRID_W = 64
ROPE_BASE = 10000.0
EPS = 1e-6
NEG = -1e30
SUBLANES = 8
ONES_ROWS = 16
AUG = D_HEAD + ONES_ROWS
ROW_BLOCK = 256
HEADS_PER_STEP = 2

VMEM_LIMIT_BYTES = 56 * 1024 * 1024

NT_DIMS = (((1,), (1,)), ((), ()))
TN_DIMS = (((0,), (0,)), ((), ()))

FM_GROUPS = ("rq", "rv", "rz", "mv", "mo", "mz", "gr", "gm")
TM_GROUPS = ("rk", "mk", "mq")

GQ_ROWK, GQ_COLQ, GQ_SC, GQ_EMM, GQ_EW, GQ_DEC = range(6)
N_GQ = 12
GT_CM, GT_TOT, GT_RMAX, GT_BCUM = range(4)


def _silu(v):
    return v * jax.nn.sigmoid(v)


def _params(*sem):
    return pltpu.CompilerParams(dimension_semantics=sem, vmem_limit_bytes=VMEM_LIMIT_BYTES)


def _adaln_kernel(c_ref, w_ref, b_ref, o_ref):
    s = _silu(c_ref[...])
    o_ref[0] = jnp.dot(s, w_ref[0], preferred_element_type=F32,
                       precision=lax.Precision.HIGHEST) + b_ref[0]


def _adaln(cc, w_ada, b_ada):
    depth, d, d3 = w_ada.shape
    rows = cc.shape[0]
    tn = 512
    return pl.pallas_call(
        _adaln_kernel,
        grid=(depth, d3 // tn),
        in_specs=[pl.BlockSpec((rows, d), lambda l, j: (0, 0)),
                  pl.BlockSpec((1, d, tn), lambda l, j: (l, 0, j)),
                  pl.BlockSpec((1, 1, tn), lambda l, j: (l, 0, j))],
        out_specs=pl.BlockSpec((1, rows, tn), lambda l, j: (l, 0, j)),
        out_shape=jax.ShapeDtypeStruct((depth, rows, d3), F32),
        compiler_params=_params("parallel", "parallel"),
        name="adaln",
    )(cc, w_ada, b_ada.reshape(depth, 1, d3))


def _norm_mod_to_scratch(ctx_ref, x_ref, nw_ref, mx_ref, mc_ref, h_scr):
    rows = ROW_BLOCK
    n_ctx = ctx_ref.shape[1]

    def block(src_ref, shift, scale, src_row, dst_row):
        xv = src_ref[0, pl.ds(src_row, rows), :]
        ms = jnp.mean(xv * xv, axis=-1, keepdims=True)
        y = xv * lax.rsqrt(ms + EPS) * nw_ref[...]
        h_scr[pl.ds(dst_row, rows), :] = (y * (1.0 + scale) + shift).astype(BF16)

    def ctx_body(i, carry):
        r0 = pl.multiple_of(i * rows, rows)
        block(ctx_ref, mc_ref[0:1, :], mc_ref[1:2, :], r0, r0)
        return carry

    def lat_body(i, carry):
        r0 = pl.multiple_of(i * rows, rows)
        block(x_ref, mx_ref[0, 0:1, :], mx_ref[0, 1:2, :], r0, pl.multiple_of(r0 + n_ctx, rows))
        return carry

    lax.fori_loop(0, n_ctx // rows, ctx_body, 0)
    lax.fori_loop(0, x_ref.shape[1] // rows, lat_body, 0)


def _proj_kernel(ctx_ref, x_ref, nw_ref, mx_ref, mc_ref, wt_ref, bt_ref, wg_ref, bg_ref, wk_ref, bk_ref,
                 cw_ref, cb_ref, ps_ref, cos_t_ref, sin_t_ref, cos_k_ref, sin_k_ref,
                 ut_ref, g_ref, uk_ref, h_scr, *, plan):
    j = pl.program_id(1)
    n_ctx = ctx_ref.shape[1]
    tn = wt_ref.shape[0]
    t = h_scr.shape[0]
    rb = ROW_BLOCK
    half = CONV_W // 2

    @pl.when(j == 0)
    def _():
        _norm_mod_to_scratch(ctx_ref, x_ref, nw_ref, mx_ref, mc_ref, h_scr)
        g = lax.dot_general(wg_ref[...], h_scr[...], NT_DIMS, preferred_element_type=F32)
        g_ref[0] = g + bg_ref[...]

    def first_row(with_ctx, out_ref, feature_major):
        if with_ctx:
            return 0
        if feature_major:
            out_ref[0, :, 0:n_ctx] = jnp.zeros((tn, n_ctx), BF16)
        else:
            out_ref[0, 0:n_ctx, :] = jnp.zeros((n_ctx, tn), BF16)
        return n_ctx

    def fm_rope(with_ctx):
        scale = D_HEAD ** -0.5
        for r0 in range(first_row(with_ctx, ut_ref, True), t, rb):
            acc = lax.dot_general(wt_ref[...], h_scr[r0:r0 + rb, :], NT_DIMS,
                                  preferred_element_type=F32) + bt_ref[...]
            if r0 >= n_ctx:
                l0 = r0 - n_ctx
                cos, sin = cos_t_ref[:, l0:l0 + rb], sin_t_ref[:, l0:l0 + rb]
                heads = [acc[hh:hh + D_HEAD] for hh in range(0, tn, D_HEAD)]
                acc = jnp.concatenate([q * cos + pltpu.roll(q, D_HEAD // 2, 0) * sin for q in heads], axis=0)
            ut_ref[0, :, r0:r0 + rb] = (acc * scale).astype(BF16)

    def fm_plain(with_ctx):
        r0 = first_row(with_ctx, ut_ref, True)
        acc = lax.dot_general(wt_ref[...], h_scr[r0:, :], NT_DIMS, preferred_element_type=F32)
        ut_ref[0, :, r0:] = (acc + bt_ref[...]).astype(BF16)

    def fm_block(r0):
        acc = lax.dot_general(wt_ref[...], h_scr[r0:r0 + rb, :], NT_DIMS, preferred_element_type=F32)
        ut_ref[0, :, r0:r0 + rb] = (acc + bt_ref[...]).astype(BF16)

    def project(r0):
        return jnp.dot(h_scr[r0:r0 + rb, :], wk_ref[...], preferred_element_type=F32) + bk_ref[...]

    def tm_rope(with_ctx):
        for r0 in range(first_row(with_ctx, uk_ref, False), t, rb):
            acc = project(r0)
            if r0 >= n_ctx:
                l0 = r0 - n_ctx
                cos, sin = cos_k_ref[l0:l0 + rb, :], sin_k_ref[l0:l0 + rb, :]
                heads = [acc[:, hh:hh + D_HEAD] for hh in range(0, tn, D_HEAD)]
                acc = jnp.concatenate([k * cos + pltpu.roll(k, D_HEAD // 2, 1) * sin for k in heads], axis=1)
            uk_ref[0, r0:r0 + rb, :] = acc.astype(BF16)

    def tm_conv(with_ctx, fm_with_ctx):
        zeros = jnp.zeros((SUBLANES, tn), F32)
        seg_edges = (0, n_ctx, t)
        raw = {}

        def conv_block(r0):
            top = zeros if r0 in seg_edges else raw[r0 - rb][rb - SUBLANES:]
            bot = zeros if r0 + rb in seg_edges else raw[r0 + rb][:SUBLANES]
            cur = raw[r0]
            ext = jnp.concatenate([top, cur, bot], axis=0)
            n_ext = rb + 2 * SUBLANES
            acc = cur * cw_ref[half:half + 1, :] + cb_ref[...]
            for jj in range(CONV_W):
                s = jj - half
                if s != 0:
                    shifted = pltpu.roll(ext, (-s) % n_ext, 0)[SUBLANES:SUBLANES + rb]
                    acc = acc + shifted * cw_ref[jj:jj + 1, :]
            uk_ref[0, r0:r0 + rb, :] = (_silu(acc) * ps_ref[...]).astype(BF16)

        tm_first = first_row(with_ctx, uk_ref, False)
        fm_first = first_row(fm_with_ctx, ut_ref, True)
        for r0 in range(0, t, rb):
            if r0 >= tm_first:
                raw[r0] = project(r0)
            if r0 >= fm_first:
                fm_block(r0)
            if r0 > tm_first:
                conv_block(r0 - rb)
        conv_block(t - rb)

    for lo, hi, fm_kind, tm_kind, fm_ctx, tm_ctx in plan:
        @pl.when((j >= lo) & (j < hi))
        def _(fm_kind=fm_kind, tm_kind=tm_kind, fm_ctx=fm_ctx, tm_ctx=tm_ctx):
            if tm_kind == "conv":
                assert fm_kind == "plain"
                tm_conv(tm_ctx, fm_ctx)
                return
            if fm_kind == "rope":
                fm_rope(fm_ctx)
            else:
                fm_plain(fm_ctx)
            if tm_kind == "rope":
                tm_rope(tm_ctx)


def _proj_plan(tiles_per_group, need_ctx_out):
    always = ("rk", "rv", "mk", "mv")
    plan = []
    for g, fm_name in enumerate(FM_GROUPS):
        tm_name = TM_GROUPS[g] if g < len(TM_GROUPS) else None
        fm_kind = "rope" if fm_name == "rq" else "plain"
        tm_kind = None if tm_name is None else ("rope" if tm_name == "rk" else "conv")
        entry = (fm_kind, tm_kind, need_ctx_out or fm_name in always, need_ctx_out or tm_name in always)
        lo = g * tiles_per_group
        if plan and plan[-1][2:] == entry:
            plan[-1] = (plan[-1][0], lo + tiles_per_group) + entry
        else:
            plan.append((lo, lo + tiles_per_group) + entry)
    return tuple(plan)


def _proj(ctx, x, nw, mx, mc, w, tables, need_ctx_out):
    b, n_lat, d = x.shape
    n_ctx = ctx.shape[1]
    t = n_ctx + n_lat
    wt, bt, wg, bg, wk, bk, cw, cb, ps = w
    n_fm = wt.shape[0]
    n_tm = wk.shape[1]
    tn = 512
    tiles_per_group = N_HEADS * D_HEAD // tn
    tm_tiles = n_tm // tn
    assert tm_tiles <= n_fm // tn and d == N_HEADS * D_HEAD
    once = pl.Buffered(1)

    def tm_blk(j):
        return jnp.minimum(j, tm_tiles - 1)

    def const(shape):
        return pl.BlockSpec(shape, lambda i, j: (0,) * len(shape), pipeline_mode=once)

    return pl.pallas_call(
        functools.partial(_proj_kernel, plan=_proj_plan(tiles_per_group, need_ctx_out)),
        grid=(b, n_fm // tn),
        in_specs=[pl.BlockSpec((1, n_ctx, d), lambda i, j: (i, 0, 0)),
                  pl.BlockSpec((1, n_lat, d), lambda i, j: (i, 0, 0)),
                  const((1, d)),
                  pl.BlockSpec((1, 3, d), lambda i, j: (i, 0, 0)),
                  const((3, d)),
                  pl.BlockSpec((tn, d), lambda i, j: (j, 0)),
                  pl.BlockSpec((tn, 1), lambda i, j: (j, 0)),
                  const(wg.shape), const(bg.shape),
                  pl.BlockSpec((d, tn), lambda i, j: (0, tm_blk(j))),
                  pl.BlockSpec((1, tn), lambda i, j: (0, tm_blk(j))),
                  pl.BlockSpec((CONV_W, tn), lambda i, j: (0, tm_blk(j))),
                  pl.BlockSpec((1, tn), lambda i, j: (0, tm_blk(j))),
                  pl.BlockSpec((1, tn), lambda i, j: (0, tm_blk(j))),
                  const(tables[0].shape), const(tables[1].shape), const(tables[2].shape), const(tables[3].shape)],
        out_specs=[pl.BlockSpec((1, tn, t), lambda i, j: (i, j, 0)),
                   pl.BlockSpec((1, wg.shape[0], t), lambda i, j: (i, 0, 0)),
                   pl.BlockSpec((1, t, tn), lambda i, j: (i, 0, tm_blk(j)))],
        out_shape=[jax.ShapeDtypeStruct((b, n_fm, t), BF16),
                   jax.ShapeDtypeStruct((b, wg.shape[0], t), F32),
                   jax.ShapeDtypeStruct((b, t, n_tm), BF16)],
        scratch_shapes=[pltpu.VMEM((t, d), BF16)],
        compiler_params=_params("parallel", "arbitrary"),
        name="proj",
    )(ctx, x, nw, mx, mc, wt, bt, wg, bg, wk, bk, cw, cb, ps, *tables)


def _chunk_start(c):
    return pl.multiple_of(c * CHUNK, CHUNK)


def _loop(lo, hi, body, reverse=False, unroll=1):
    n = hi - lo
    if n <= 0:
        return

    def wrapped(i, carry):
        body(hi - 1 - i if reverse else lo + i)
        return carry

    lax.fori_loop(0, n, wrapped, 0, unroll=unroll)


def _unroll(n):
    return max(k for k in (1, 2, 3, 4, 6, 8, 9) if n % k == 0)


def _scan(lo, hi, body, init, reverse=False):
    n = hi - lo
    if n <= 0:
        return init

    def wrapped(i, carry):
        return body(hi - 1 - i if reverse else lo + i, carry)

    return lax.fori_loop(0, n, wrapped, init)


def _head_norm_gate(o, nw_col, z):
    mu = jnp.mean(o, axis=0, keepdims=True)
    oc = o - mu
    var = jnp.mean(oc * oc, axis=0, keepdims=True)
    return oc * lax.rsqrt(var + EPS) * nw_col * _silu(z)


def _ret_kernel(lg_ref, qt_ref, k_ref, vt_ref, zt_ref, nw_ref, y_ref, kv, st, pt_scr, cr_scr,
                *, n_ctx_chunks, n_chunks, out_ctx, heads_per_step):
    for hh in range(heads_per_step):
        _ret_head(lg_ref, qt_ref, k_ref, vt_ref, zt_ref, nw_ref, y_ref, kv, st, pt_scr, cr_scr,
                  pl.program_id(1) * heads_per_step + hh, hh * D_HEAD, n_ctx_chunks, n_chunks, out_ctx)


def _ret_head(lg_ref, qt_ref, k_ref, vt_ref, zt_ref, nw_ref, y_ref, kv, st, pt_scr, cr_scr,
              h, r0, n_ctx_chunks, n_chunks, out_ctx):
    c_ = CHUNK
    hs = slice(r0, r0 + D_HEAD)
    lg_f = lg_ref[0, h]
    lg_b = lg_ref[1, h]
    lane = lax.broadcasted_iota(jnp.int32, (1, c_), 1).astype(F32)
    zeta_f = jnp.exp(lg_f * (c_ - 1.0 - lane)).astype(BF16)
    zeta_b = jnp.exp(lg_b * lane).astype(BF16)
    xi_f = jnp.exp(lg_f * (lane + 1.0))
    xi_b = jnp.exp(lg_b * (c_ - lane))
    dec_f = jnp.exp(lg_f * jnp.full((1, c_), float(c_), F32))
    dec_b = jnp.exp(lg_b * jnp.full((1, c_), float(c_), F32))
    ki = lax.broadcasted_iota(jnp.int32, (c_, c_), 0)
    qi = lax.broadcasted_iota(jnp.int32, (c_, c_), 1)
    diff = (qi - ki).astype(F32)
    dsum = (jnp.where(diff >= 0, jnp.exp(lg_f * jnp.maximum(diff, 0.0)), 0.0)
            + jnp.where(diff <= 0, jnp.exp(lg_b * jnp.maximum(-diff, 0.0)), 0.0))

    def outer_products(c):
        t0 = _chunk_start(c)
        vt = vt_ref[0, hs, pl.ds(t0, c_)]
        lhs = jnp.concatenate([vt * zeta_f, vt * zeta_b], axis=0)
        kv[c] = jnp.dot(lhs, k_ref[0, pl.ds(t0, c_), hs], preferred_element_type=F32)

    _loop(0, n_chunks, outer_products, unroll=_unroll(n_chunks))

    def scan_f(c, s):
        st[c, 0:c_, :] = s.astype(BF16)
        return dec_f * s + kv[c, 0:c_, :]

    def scan_b(c, s):
        st[c, c_:2 * c_, :] = s.astype(BF16)
        return dec_b * s + kv[c, c_:2 * c_, :]

    zero = jnp.zeros((c_, c_), F32)
    _scan(0, n_chunks, scan_f, zero)
    sb = _scan(0, n_ctx_chunks, scan_b, zero, reverse=True)
    _scan(n_ctx_chunks, n_chunks, scan_b, sb, reverse=True)

    nw_col = nw_ref[hs, :]

    def scores(c):
        t0 = _chunk_start(c)
        qt = qt_ref[0, hs, pl.ds(t0, c_)]
        at = jnp.dot(k_ref[0, pl.ds(t0, c_), hs], qt, preferred_element_type=F32)
        crs = jnp.dot(st[c], qt, preferred_element_type=F32)
        pt_scr[c] = (at * dsum).astype(BF16)
        cr_scr[c] = crs[0:c_, :] * xi_f + crs[c_:2 * c_, :] * xi_b

    def outputs(c):
        t0 = _chunk_start(c)
        o = jnp.dot(vt_ref[0, hs, pl.ds(t0, c_)], pt_scr[c], preferred_element_type=F32) + cr_scr[c]
        z = zt_ref[0, hs, pl.ds(t0, c_)].astype(F32)
        y_ref[0, hs, pl.ds(t0, c_)] = _head_norm_gate(o, nw_col, z).astype(BF16)

    lo = 0 if out_ctx else n_ctx_chunks
    if not out_ctx:
        y_ref[0, hs, 0:n_ctx_chunks * c_] = jnp.zeros((D_HEAD, n_ctx_chunks * c_), BF16)
    _loop(lo, n_chunks, scores, unroll=_unroll(n_chunks - lo))
    _loop(lo, n_chunks, outputs, unroll=_unroll(n_chunks - lo))


def _ret_mix(lg, ut, uk, nw, n_ctx, out_ctx):
    b, _, t = ut.shape
    n_chunks = t // CHUNK
    hd = D_HEAD
    hps = HEADS_PER_STEP
    blk = N_HEADS // hps

    def fm(name):
        g = FM_GROUPS.index(name)
        return pl.BlockSpec((1, hps * hd, t), lambda i, h: (i, g * blk + h, 0))

    tm_rk = TM_GROUPS.index("rk")
    kern = functools.partial(_ret_kernel, n_ctx_chunks=n_ctx // CHUNK, n_chunks=n_chunks, out_ctx=out_ctx,
                             heads_per_step=hps)
    return pl.pallas_call(
        kern,
        grid=(b, blk),
        in_specs=[pl.BlockSpec(memory_space=pltpu.SMEM),
                  fm("rq"),
                  pl.BlockSpec((1, t, hps * hd), lambda i, h: (i, 0, tm_rk * blk + h)),
                  fm("rv"), fm("rz"),
                  pl.BlockSpec((hps * hd, 1), lambda i, h: (h, 0))],
        out_specs=pl.BlockSpec((1, hps * hd, t), lambda i, h: (i, h, 0)),
        out_shape=jax.ShapeDtypeStruct((b, N_HEADS * hd, t), BF16),
        scratch_shapes=[pltpu.VMEM((n_chunks, 2 * hd, hd), F32),
                        pltpu.VMEM((n_chunks, 2 * hd, hd), BF16),
                        pltpu.VMEM((n_chunks, hd, hd), BF16),
                        pltpu.VMEM((n_chunks, hd, hd), F32)],
        compiler_params=_params("parallel", "arbitrary"),
        name="ret_mix",
    )(lg, ut, uk, ut, ut, nw)


def _seg_scan(v, pos, op, fill, reverse):
    t = v.shape[1]
    step = 1
    while step < CHUNK:
        if reverse:
            v = op(v, jnp.where(pos < CHUNK - step, pltpu.roll(v, t - step, 1), fill))
        else:
            v = op(v, jnp.where(pos >= step, pltpu.roll(v, step, 1), fill))
        step *= 2
    return v


def _gate_kernel(g_ref, o_ref, tmp, *, n_ctx_chunks, n_chunks):
    c_ = CHUNK
    nh = N_HEADS
    g = g_ref[0]
    pos = lax.broadcasted_iota(jnp.int32, (nh, g.shape[1]), 1) & (c_ - 1)
    for d in range(2):
        rev = d == 1
        ig = g[2 * d * nh:(2 * d + 1) * nh]
        lf = jax.nn.log_sigmoid(g[(2 * d + 1) * nh:(2 * d + 2) * nh])
        bcum = _seg_scan(lf, pos, jnp.add, 0.0, rev)
        tot = bcum + _seg_scan(lf, pos, jnp.add, 0.0, not rev) - lf
        r = ig - bcum
        cm = _seg_scan(r, pos, jnp.maximum, NEG, rev)
        rmax = jnp.maximum(cm, _seg_scan(r, pos, jnp.maximum, NEG, not rev))
        o_ref[0, 2 * GQ_ROWK + d] = r
        tmp[2 * GT_CM + d] = cm
        tmp[2 * GT_TOT + d] = tot
        tmp[2 * GT_RMAX + d] = rmax
        tmp[2 * GT_BCUM + d] = bcum

    def make_step(d):
        def step(c, m_prev):
            sl = pl.ds(_chunk_start(c), c_)
            gt = tmp[2 * GT_TOT + d, :, sl]
            m_new = jnp.maximum(gt + m_prev, gt + tmp[2 * GT_RMAX + d, :, sl])
            mx = jnp.maximum(tmp[2 * GT_CM + d, :, sl], m_prev)
            o_ref[0, 2 * GQ_COLQ + d, :, sl] = -mx
            o_ref[0, 2 * GQ_SC + d, :, sl] = jnp.exp(m_prev - mx)
            o_ref[0, 2 * GQ_EMM + d, :, sl] = jnp.exp(-(tmp[2 * GT_BCUM + d, :, sl] + mx))
            o_ref[0, 2 * GQ_EW + d, :, sl] = jnp.exp(gt + o_ref[0, 2 * GQ_ROWK + d, :, sl] - m_new)
            o_ref[0, 2 * GQ_DEC + d, :, sl] = jnp.exp(gt + m_prev - m_new)
            return m_new
        return step

    m0 = jnp.zeros((nh, c_), F32)
    _scan(0, n_chunks, make_step(0), m0)
    mb = _scan(0, n_ctx_chunks, make_step(1), m0, reverse=True)
    _scan(n_ctx_chunks, n_chunks, make_step(1), mb, reverse=True)


def _gate_rows(gt, n_ctx):
    b, n_rows, t = gt.shape
    kern = functools.partial(_gate_kernel, n_ctx_chunks=n_ctx // CHUNK, n_chunks=t // CHUNK)
    rows = pl.pallas_call(
        kern,
        grid=(b,),
        in_specs=[pl.BlockSpec((1, n_rows, t), lambda i: (i, 0, 0))],
        out_specs=pl.BlockSpec((1, N_GQ, N_HEADS, t), lambda i: (i, 0, 0, 0)),
        out_shape=jax.ShapeDtypeStruct((b, N_GQ, N_HEADS, t), F32),
        scratch_shapes=[pltpu.VMEM((8, N_HEADS, t), F32)],
        compiler_params=_params("parallel"),
        name="ml_gates",
    )(gt)
    return rows.transpose(0, 2, 1, 3)


def _ml_kernel(q_ref, k_ref, vt_ref, ot_ref, zt_ref, gs_ref, nw_ref, y_ref, kv, st, pt_scr, cr_scr,
               *, n_ctx_chunks, n_chunks, out_ctx, heads_per_step):
    for hh in range(heads_per_step):
        _ml_head(q_ref, k_ref, vt_ref, ot_ref, zt_ref, gs_ref, nw_ref, y_ref, kv, st, pt_scr, cr_scr,
                 hh, n_ctx_chunks, n_chunks, out_ctx)


def _ml_head(q_ref, k_ref, vt_ref, ot_ref, zt_ref, gs_ref, nw_ref, y_ref, kv, st, pt_scr, cr_scr,
             hh, n_ctx_chunks, n_chunks, out_ctx):
    c_ = CHUNK
    hs = slice(hh * D_HEAD, (hh + 1) * D_HEAD)

    def grow(q, d, sl):
        return gs_ref[0, hh, 2 * q + d:2 * q + d + 1, sl]

    ones = jnp.ones((ONES_ROWS, c_), BF16)

    def outer_products(c):
        sl = pl.ds(_chunk_start(c), c_)
        vt = vt_ref[0, hs, sl]
        ew_f = grow(GQ_EW, 0, sl).astype(BF16)
        ew_b = grow(GQ_EW, 1, sl).astype(BF16)
        lhs = jnp.concatenate([vt * ew_f, ones * ew_f, vt * ew_b, ones * ew_b], axis=0)
        kv[c] = jnp.dot(lhs, k_ref[0, sl, hs], preferred_element_type=F32)

    _loop(0, n_chunks, outer_products, unroll=_unroll(n_chunks))

    def make_scan(d):
        def step(c, s):
            sl = pl.ds(_chunk_start(c), c_)
            st[c, d * AUG:(d + 1) * AUG, :] = s.astype(BF16)
            return grow(GQ_DEC, d, sl) * s + kv[c, d * AUG:(d + 1) * AUG, :]
        return step

    zero = jnp.zeros((AUG, c_), F32)
    _scan(0, n_chunks, make_scan(0), zero)
    sb = _scan(0, n_ctx_chunks, make_scan(1), zero, reverse=True)
    _scan(n_ctx_chunks, n_chunks, make_scan(1), sb, reverse=True)

    nw_col = nw_ref[hs, :]
    ki = lax.broadcasted_iota(jnp.int32, (c_, c_), 0)
    qi = lax.broadcasted_iota(jnp.int32, (c_, c_), 1)

    def scores(c):
        sl = pl.ds(_chunk_start(c), c_)
        qc = q_ref[0, sl, hs]
        at = lax.dot_general(k_ref[0, sl, hs], qc, NT_DIMS, preferred_element_type=F32)
        crs = lax.dot_general(st[c], qc, NT_DIMS, preferred_element_type=F32)
        rowk_f = jnp.broadcast_to(grow(GQ_ROWK, 0, sl), (c_, c_)).T
        rowk_b = jnp.broadcast_to(grow(GQ_ROWK, 1, sl), (c_, c_)).T
        s_f = at * jnp.exp(jnp.where(ki <= qi, rowk_f + grow(GQ_COLQ, 0, sl), NEG))
        s_b = at * jnp.exp(jnp.where(ki >= qi, rowk_b + grow(GQ_COLQ, 1, sl), NEG))
        sc_f = grow(GQ_SC, 0, sl)
        sc_b = grow(GQ_SC, 1, sl)
        den_f = jnp.sum(s_f, axis=0, keepdims=True) + crs[D_HEAD:D_HEAD + 1, :] * sc_f
        den_b = jnp.sum(s_b, axis=0, keepdims=True) + crs[AUG + D_HEAD:AUG + D_HEAD + 1, :] * sc_b
        r_f = 1.0 / jnp.maximum(jnp.abs(den_f), grow(GQ_EMM, 0, sl))
        r_b = 1.0 / jnp.maximum(jnp.abs(den_b), grow(GQ_EMM, 1, sl))
        pt_scr[c] = (s_f * r_f + s_b * r_b).astype(BF16)
        cr_scr[c] = crs[0:D_HEAD, :] * (sc_f * r_f) + crs[AUG:AUG + D_HEAD, :] * (sc_b * r_b)

    def outputs(c):
        sl = pl.ds(_chunk_start(c), c_)
        o = jnp.dot(vt_ref[0, hs, sl], pt_scr[c], preferred_element_type=F32) + cr_scr[c]
        o = jax.nn.sigmoid(ot_ref[0, hs, sl].astype(F32)) * o
        z = zt_ref[0, hs, sl].astype(F32)
        y_ref[0, hs, sl] = _head_norm_gate(o, nw_col, z).astype(BF16)

    lo = 0 if out_ctx else n_ctx_chunks
    if not out_ctx:
        y_ref[0, hs, 0:n_ctx_chunks * c_] = jnp.zeros((D_HEAD, n_ctx_chunks * c_), BF16)
    _loop(lo, n_chunks, scores, unroll=_unroll(n_chunks - lo))
    _loop(lo, n_chunks, outputs, unroll=_unroll(n_chunks - lo))


def _ml_mix(ut, uk, gs, nw, n_ctx, out_ctx):
    b, _, t = ut.shape
    n_chunks = t // CHUNK
    hd = D_HEAD
    hps = HEADS_PER_STEP
    blk = N_HEADS // hps

    def fm(name):
        g = FM_GROUPS.index(name)
        return pl.BlockSpec((1, hps * hd, t), lambda i, h: (i, g * blk + h, 0))

    def tm(name):
        g = TM_GROUPS.index(name)
        return pl.BlockSpec((1, t, hps * hd), lambda i, h: (i, 0, g * blk + h))

    kern = functools.partial(_ml_kernel, n_ctx_chunks=n_ctx // CHUNK, n_chunks=n_chunks, out_ctx=out_ctx,
                             heads_per_step=hps)
    return pl.pallas_call(
        kern,
        grid=(b, blk),
        in_specs=[tm("mq"), tm("mk"), fm("mv"), fm("mo"), fm("mz"),
                  pl.BlockSpec((1, hps, N_GQ, t), lambda i, h: (i, h, 0, 0)),
                  pl.BlockSpec((hps * hd, 1), lambda i, h: (h, 0))],
        out_specs=pl.BlockSpec((1, hps * hd, t), lambda i, h: (i, h, 0)),
        out_shape=jax.ShapeDtypeStruct((b, N_HEADS * hd, t), BF16),
        scratch_shapes=[pltpu.VMEM((n_chunks, 2 * AUG, hd), F32),
                        pltpu.VMEM((n_chunks, 2 * AUG, hd), BF16),
                        pltpu.VMEM((n_chunks, hd, hd), BF16),
                        pltpu.VMEM((n_chunks, hd, hd), F32)],
        compiler_params=_params("parallel", "arbitrary"),
        name="ml_mix",
    )(uk, uk, ut, ut, ut, gs, nw)


def _out_kernel(yr_ref, ym_ref, gr_ref, gm_ref, wro_ref, wmo_ref, wout_ref, x_ref, gate_ref, fnw_ref,
                o_ref, *, final):
    br = jnp.dot(wro_ref[...], yr_ref[0], preferred_element_type=F32)
    bm = jnp.dot(wmo_ref[...], ym_ref[0], preferred_element_type=F32)
    y = (jax.nn.sigmoid(gr_ref[0].astype(F32)) * br + jax.nn.sigmoid(gm_ref[0].astype(F32)) * bm).astype(BF16)
    o = lax.dot_general(y, wout_ref[...], TN_DIMS, preferred_element_type=F32)
    xn = x_ref[0] + gate_ref[0] * o
    if final:
        ms = jnp.mean(xn * xn, axis=-1, keepdims=True)
        xn = xn * lax.rsqrt(ms + EPS) * fnw_ref[...]
    o_ref[0] = xn


def _out(yr, ym, ut, wro_t, wmo_t, wout, xs, gate, fnw, tok0, final):
    b, n, d = xs.shape
    tm = 2 * ROW_BLOCK if n % (2 * ROW_BLOCK) == 0 else ROW_BLOCK
    assert n % tm == 0
    gr_row = FM_GROUPS.index("gr") * N_HEADS * D_HEAD
    gm_row = FM_GROUPS.index("gm") * N_HEADS * D_HEAD

    def tok(row0):
        return pl.BlockSpec((pl.Element(1), pl.Element(d), pl.Element(tm)), lambda i, j: (i, pl.multiple_of(j * 0 + row0, 128), pl.multiple_of(tok0 + j * tm, 128)))

    kern = functools.partial(_out_kernel, final=final)
    return pl.pallas_call(
        kern,
        grid=(b, n // tm),
        in_specs=[tok(0), tok(0), tok(gr_row), tok(gm_row),
                  pl.BlockSpec((d, d), lambda i, j: (0, 0)),
                  pl.BlockSpec((d, d), lambda i, j: (0, 0)),
                  pl.BlockSpec((d, d), lambda i, j: (0, 0)),
                  pl.BlockSpec((1, tm, d), lambda i, j: (i, j, 0)),
                  pl.BlockSpec((1, 1, d), lambda i, j: (i, 0, 0)),
                  pl.BlockSpec((1, d), lambda i, j: (0, 0))],
        out_specs=pl.BlockSpec((1, tm, d), lambda i, j: (i, j, 0)),
        out_shape=jax.ShapeDtypeStruct((b, n, d), F32),
        compiler_params=_params("parallel", "parallel"),
        name="out_final" if final else "out",
    )(yr, ym, ut, ut, wro_t, wmo_t, wout, xs, gate, fnw)


def _rope_tables(n_lat):
    rows_n = n_lat // GRID_W
    rows = jnp.repeat(jnp.arange(rows_n, dtype=F32), GRID_W)
    cols = jnp.tile(jnp.arange(GRID_W, dtype=F32), rows_n)
    nf = D_HEAD // 4
    freqs = ROPE_BASE ** (-jnp.arange(nf, dtype=F32) / nf)
    ang = jnp.concatenate([rows[:, None] * freqs, cols[:, None] * freqs], axis=-1)
    cos, sin = jnp.cos(ang), jnp.sin(ang)
    cos_k = jnp.concatenate([cos, cos], axis=-1)
    sin_k = jnp.concatenate([-sin, sin], axis=-1)
    return cos_k.T, sin_k.T, cos_k, sin_k


def _split_weights(w_in, b_in, conv_w, conv_b):
    hd = N_HEADS * D_HEAD
    d = w_in.shape[0]
    off = {name: i * hd for i, name in enumerate(("rq", "rk", "rv", "rz", "mq", "mk", "mv", "mo", "mz"))}
    g_off = 9 * hd
    n_g = 4 * N_HEADS
    off["gr"] = g_off + n_g
    off["gm"] = g_off + n_g + d
    widths = dict.fromkeys(off, hd)
    widths["gr"] = widths["gm"] = d

    def cols(a, name):
        return a[..., off[name]:off[name] + widths[name]]

    wt = jnp.concatenate([cols(w_in, n) for n in FM_GROUPS], axis=-1).T.astype(BF16)
    bt = jnp.concatenate([cols(b_in, n) for n in FM_GROUPS], axis=-1)[:, None]
    wk = jnp.concatenate([cols(w_in, n) for n in TM_GROUPS], axis=-1).astype(BF16)
    bk = jnp.concatenate([cols(b_in, n) for n in TM_GROUPS], axis=-1)[None, :]
    wg = w_in[:, g_off:g_off + n_g].T.astype(BF16)
    bg = b_in[g_off:g_off + n_g][:, None]
    conv_cols = {"mq": slice(0, hd), "mk": slice(hd, 2 * hd)}
    cw = jnp.concatenate([conv_w[:, conv_cols[n]] if n in conv_cols else jnp.zeros((CONV_W, hd), F32)
                          for n in TM_GROUPS], axis=-1)
    cb = jnp.concatenate([conv_b[conv_cols[n]] if n in conv_cols else jnp.zeros((hd,), F32)
                          for n in TM_GROUPS])[None, :]
    ps = jnp.concatenate([jnp.full((hd,), D_HEAD ** -0.5 if n == "mq" else 1.0, F32) for n in TM_GROUPS])[None, :]
    return wt, bt, wg, bg, wk, bk, cw, cb, ps


def kernel(x, c, ctx, c_ctx, norm_w, w_ada, b_ada, w_in, b_in, conv_w, conv_b, ret_log_gamma, ret_norm_w,
           ml_norm_w, w_ret_o, w_ml_o, w_out, final_norm_w):
    b, n_lat, d = x.shape
    n_ctx = ctx.shape[1]
    depth = w_in.shape[0]
    hd = N_HEADS * D_HEAD
    assert d == hd and n_lat % ROW_BLOCK == 0 and n_ctx % ROW_BLOCK == 0 and n_lat % GRID_W == 0
    assert TM_GROUPS[0] == "rk" and FM_GROUPS[0] == "rq"

    rows = -(-(b + 1) // SUBLANES) * SUBLANES
    cc = jnp.zeros((rows, d), F32).at[:b].set(c).at[b].set(c_ctx)
    mods = _adaln(cc, w_ada, b_ada)
    tables = _rope_tables(n_lat)

    for l in range(depth):
        final = l == depth - 1
        mx = mods[l, :b].reshape(b, 3, d)
        mc = mods[l, b].reshape(3, d)
        w = _split_weights(w_in[l], b_in[l], conv_w[l], conv_b[l])
        ut, gt, uk = _proj(ctx, x, norm_w[l][None, :], mx, mc, w, tables, not final)
        yr = _ret_mix(ret_log_gamma[l], ut, uk, ret_norm_w[l][:, None], n_ctx, not final)
        gs = _gate_rows(gt, n_ctx)
        ym = _ml_mix(ut, uk, gs, ml_norm_w[l][:, None], n_ctx, not final)
        wo = (w_ret_o[l].T.astype(BF16), w_ml_o[l].T.astype(BF16), w_out[l].astype(BF16))
        fnw = final_norm_w[None, :]
        if not final:
            ctx = _out(yr, ym, ut, *wo, ctx, jnp.broadcast_to(mc[2], (b, 1, d)), fnw, 0, False)
        x = _out(yr, ym, ut, *wo, x, mx[:, 2:3], fnw, n_ctx, final)
    return x
```

```python
import functools

import jax
import jax.numpy as jnp
from jax import lax
from jax.experimental import pallas as pl
from jax.experimental.pallas import tpu as pltpu

F32 = jnp.float32
BF16 = jnp.bfloat16

N_HEADS = 8
D_HEAD = 128
CHUNK = 128
CONV_W = 5
GRID_W = 64
ROPE_BASE = 10000.0
EPS = 1e-6
NEG = -1e30
SUBLANES = 8
ONES_ROWS = 16
AUG = D_HEAD + ONES_ROWS
ROW_BLOCK = 256
HEADS_PER_STEP = 2

VMEM_LIMIT_BYTES = 56 * 1024 * 1024

NT_DIMS = (((1,), (1,)), ((), ()))
TN_DIMS = (((0,), (0,)), ((), ()))

FM_GROUPS = ("rq", "rv", "rz", "mv", "mo", "mz", "gr", "gm")
TM_GROUPS = ("rk", "mk", "mq")

GQ_ROWK, GQ_COLQ, GQ_SC, GQ_EMM, GQ_EW, GQ_DEC = range(6)
N_GQ = 12
GT_CM, GT_TOT, GT_RMAX, GT_BCUM = range(4)


def _silu(v):
    return v * jax.nn.sigmoid(v)


def _params(*sem):
    return pltpu.CompilerParams(dimension_semantics=sem, vmem_limit_bytes=VMEM_LIMIT_BYTES)


def _adaln_kernel(c_ref, w_ref, b_ref, o_ref):
    s = _silu(c_ref[...])
    o_ref[0] = jnp.dot(s, w_ref[0], preferred_element_type=F32,
                       precision=lax.Precision.HIGHEST) + b_ref[0]


def _adaln(cc, w_ada, b_ada):
    depth, d, d3 = w_ada.shape
    rows = cc.shape[0]
    tn = 512
    return pl.pallas_call(
        _adaln_kernel,
        grid=(depth, d3 // tn),
        in_specs=[pl.BlockSpec((rows, d), lambda l, j: (0, 0)),
                  pl.BlockSpec((1, d, tn), lambda l, j: (l, 0, j)),
                  pl.BlockSpec((1, 1, tn), lambda l, j: (l, 0, j))],
        out_specs=pl.BlockSpec((1, rows, tn), lambda l, j: (l, 0, j)),
        out_shape=jax.ShapeDtypeStruct((depth, rows, d3), F32),
        compiler_params=_params("parallel", "parallel"),
        name="adaln",
    )(cc, w_ada, b_ada.reshape(depth, 1, d3))


def _norm_mod_to_scratch(ctx_ref, x_ref, nw_ref, mx_ref, mc_ref, h_scr):
    rows = ROW_BLOCK
    n_ctx = ctx_ref.shape[1]

    def block(src_ref, shift, scale, src_row, dst_row):
        xv = src_ref[0, pl.ds(src_row, rows), :]
        ms = jnp.mean(xv * xv, axis=-1, keepdims=True)
        y = xv * lax.rsqrt(ms + EPS) * nw_ref[...]
        h_scr[pl.ds(dst_row, rows), :] = (y * (1.0 + scale) + shift).astype(BF16)

    def ctx_body(i, carry):
        r0 = pl.multiple_of(i * rows, rows)
        block(ctx_ref, mc_ref[0:1, :], mc_ref[1:2, :], r0, r0)
        return carry

    def lat_body(i, carry):
        r0 = pl.multiple_of(i * rows, rows)
        block(x_ref, mx_ref[0, 0:1, :], mx_ref[0, 1:2, :], r0, pl.multiple_of(r0 + n_ctx, rows))
        return carry

    lax.fori_loop(0, n_ctx // rows, ctx_body, 0)
    lax.fori_loop(0, x_ref.shape[1] // rows, lat_body, 0)


def _proj_kernel(ctx_ref, x_ref, nw_ref, mx_ref, mc_ref, wt_ref, bt_ref, wg_ref, bg_ref, wk_ref, bk_ref,
                 cw_ref, cb_ref, ps_ref, cos_t_ref, sin_t_ref, cos_k_ref, sin_k_ref,
                 ut_ref, g_ref, uk_ref, h_scr, *, plan):
    j = pl.program_id(1)
    n_ctx = ctx_ref.shape[1]
    tn = wt_ref.shape[0]
    t = h_scr.shape[0]
    rb = ROW_BLOCK
    half = CONV_W // 2

    @pl.when(j == 0)
    def _():
        _norm_mod_to_scratch(ctx_ref, x_ref, nw_ref, mx_ref, mc_ref, h_scr)
        g = lax.dot_general(wg_ref[...], h_scr[...], NT_DIMS, preferred_element_type=F32)
        g_ref[0] = g + bg_ref[...]

    def first_row(with_ctx, out_ref, feature_major):
        if with_ctx:
            return 0
        if feature_major:
            out_ref[0, :, 0:n_ctx] = jnp.zeros((tn, n_ctx), BF16)
        else:
            out_ref[0, 0:n_ctx, :] = jnp.zeros((n_ctx, tn), BF16)
        return n_ctx

    def fm_rope(with_ctx):
        scale = D_HEAD ** -0.5
        for r0 in range(first_row(with_ctx, ut_ref, True), t, rb):
            acc = lax.dot_general(wt_ref[...], h_scr[r0:r0 + rb, :], NT_DIMS,
                                  preferred_element_type=F32) + bt_ref[...]
            if r0 >= n_ctx:
                l0 = r0 - n_ctx
                cos, sin = cos_t_ref[:, l0:l0 + rb], sin_t_ref[:, l0:l0 + rb]
                heads = [acc[hh:hh + D_HEAD] for hh in range(0, tn, D_HEAD)]
                acc = jnp.concatenate([q * cos + pltpu.roll(q, D_HEAD // 2, 0) * sin for q in heads], axis=0)
            ut_ref[0, :, r0:r0 + rb] = (acc * scale).astype(BF16)

    def fm_plain(with_ctx):
        r0 = first_row(with_ctx, ut_ref, True)
        acc = lax.dot_general(wt_ref[...], h_scr[r0:, :], NT_DIMS, preferred_element_type=F32)
        ut_ref[0, :, r0:] = (acc + bt_ref[...]).astype(BF16)

    def fm_block(r0):
        acc = lax.dot_general(wt_ref[...], h_scr[r0:r0 + rb, :], NT_DIMS, preferred_element_type=F32)
        ut_ref[0, :, r0:r0 + rb] = (acc + bt_ref[...]).astype(BF16)

    def project(r0):
        return jnp.dot(h_scr[r0:r0 + rb, :], wk_ref[...], preferred_element_type=F32) + bk_ref[...]

    def tm_rope(with_ctx):
        for r0 in range(first_row(with_ctx, uk_ref, False), t, rb):
            acc = project(r0)
            if r0 >= n_ctx:
                l0 = r0 - n_ctx
                cos, sin = cos_k_ref[l0:l0 + rb, :], sin_k_ref[l0:l0 + rb, :]
                heads = [acc[:, hh:hh + D_HEAD] for hh in range(0, tn, D_HEAD)]
                acc = jnp.concatenate([k * cos + pltpu.roll(k, D_HEAD // 2, 1) * sin for k in heads], axis=1)
            uk_ref[0, r0:r0 + rb, :] = acc.astype(BF16)

    def tm_conv(with_ctx, fm_with_ctx):
        zeros = jnp.zeros((SUBLANES, tn), F32)
        seg_edges = (0, n_ctx, t)
        raw = {}

        def conv_block(r0):
            top = zeros if r0 in seg_edges else raw[r0 - rb][rb - SUBLANES:]
            bot = zeros if r0 + rb in seg_edges else raw[r0 + rb][:SUBLANES]
            cur = raw[r0]
            ext = jnp.concatenate([top, cur, bot], axis=0)
            n_ext = rb + 2 * SUBLANES
            acc = cur * cw_ref[half:half + 1, :] + cb_ref[...]
            for jj in range(CONV_W):
                s = jj - half
                if s != 0:
                    shifted = pltpu.roll(ext, (-s) % n_ext, 0)[SUBLANES:SUBLANES + rb]
                    acc = acc + shifted * cw_ref[jj:jj + 1, :]
            uk_ref[0, r0:r0 + rb, :] = (_silu(acc) * ps_ref[...]).astype(BF16)

        tm_first = first_row(with_ctx, uk_ref, False)
        fm_first = first_row(fm_with_ctx, ut_ref, True)
        for r0 in range(0, t, rb):
            if r0 >= tm_first:
                raw[r0] = project(r0)
            if r0 >= fm_first:
                fm_block(r0)
            if r0 > tm_first:
                conv_block(r0 - rb)
        conv_block(t - rb)

    for lo, hi, fm_kind, tm_kind, fm_ctx, tm_ctx in plan:
        @pl.when((j >= lo) & (j < hi))
        def _(fm_kind=fm_kind, tm_kind=tm_kind, fm_ctx=fm_ctx, tm_ctx=tm_ctx):
            if tm_kind == "conv":
                assert fm_kind == "plain"
                tm_conv(tm_ctx, fm_ctx)
                return
            if fm_kind == "rope":
                fm_rope(fm_ctx)
            else:
                fm_plain(fm_ctx)
            if tm_kind == "rope":
                tm_rope(tm_ctx)


def _proj_plan(tiles_per_group, need_ctx_out):
    always = ("rk", "rv", "mk", "mv")
    plan = []
    for g, fm_name in enumerate(FM_GROUPS):
        tm_name = TM_GROUPS[g] if g < len(TM_GROUPS) else None
        fm_kind = "rope" if fm_name == "rq" else "plain"
        tm_kind = None if tm_name is None else ("rope" if tm_name == "rk" else "conv")
        entry = (fm_kind, tm_kind, need_ctx_out or fm_name in always, need_ctx_out or tm_name in always)
        lo = g * tiles_per_group
        if plan and plan[-1][2:] == entry:
            plan[-1] = (plan[-1][0], lo + tiles_per_group) + entry
        else:
            plan.append((lo, lo + tiles_per_group) + entry)
    return tuple(plan)


def _proj(ctx, x, nw, mx, mc, w, tables, need_ctx_out):
    b, n_lat, d = x.shape
    n_ctx = ctx.shape[1]
    t = n_ctx + n_lat
    wt, bt, wg, bg, wk, bk, cw, cb, ps = w
    n_fm = wt.shape[0]
    n_tm = wk.shape[1]
    tn = 512
    tiles_per_group = N_HEADS * D_HEAD // tn
    tm_tiles = n_tm // tn
    assert tm_tiles <= n_fm // tn and d == N_HEADS * D_HEAD
    once = pl.Buffered(1)

    def tm_blk(j):
        return jnp.minimum(j, tm_tiles - 1)

    def const(shape):
        return pl.BlockSpec(shape, lambda i, j: (0,) * len(shape), pipeline_mode=once)

    return pl.pallas_call(
        functools.partial(_proj_kernel, plan=_proj_plan(tiles_per_group, need_ctx_out)),
        grid=(b, n_fm // tn),
        in_specs=[pl.BlockSpec((1, n_ctx, d), lambda i, j: (i, 0, 0)),
                  pl.BlockSpec((1, n_lat, d), lambda i, j: (i, 0, 0)),
                  const((1, d)),
                  pl.BlockSpec((1, 3, d), lambda i, j: (i, 0, 0)),
                  const((3, d)),
                  pl.BlockSpec((tn, d), lambda i, j: (j, 0)),
                  pl.BlockSpec((tn, 1), lambda i, j: (j, 0)),
                  const(wg.shape), const(bg.shape),
                  pl.BlockSpec((d, tn), lambda i, j: (0, tm_blk(j))),
                  pl.BlockSpec((1, tn), lambda i, j: (0, tm_blk(j))),
                  pl.BlockSpec((CONV_W, tn), lambda i, j: (0, tm_blk(j))),
                  pl.BlockSpec((1, tn), lambda i, j: (0, tm_blk(j))),
                  pl.BlockSpec((1, tn), lambda i, j: (0, tm_blk(j))),
                  const(tables[0].shape), const(tables[1].shape), const(tables[2].shape), const(tables[3].shape)],
        out_specs=[pl.BlockSpec((1, tn, t), lambda i, j: (i, j, 0)),
                   pl.BlockSpec((1, wg.shape[0], t), lambda i, j: (i, 0, 0)),
                   pl.BlockSpec((1, t, tn), lambda i, j: (i, 0, tm_blk(j)))],
        out_shape=[jax.ShapeDtypeStruct((b, n_fm, t), BF16),
                   jax.ShapeDtypeStruct((b, wg.shape[0], t), F32),
                   jax.ShapeDtypeStruct((b, t, n_tm), BF16)],
        scratch_shapes=[pltpu.VMEM((t, d), BF16)],
        compiler_params=_params("parallel", "arbitrary"),
        name="proj",
    )(ctx, x, nw, mx, mc, wt, bt, wg, bg, wk, bk, cw, cb, ps, *tables)


def _chunk_start(c):
    return pl.multiple_of(c * CHUNK, CHUNK)


def _loop(lo, hi, body, reverse=False, unroll=1):
    n = hi - lo
    if n <= 0:
        return

    def wrapped(i, carry):
        body(hi - 1 - i if reverse else lo + i)
        return carry

    lax.fori_loop(0, n, wrapped, 0, unroll=unroll)


def _unroll(n):
    return max(k for k in (1, 2, 3, 4, 6, 8, 9, 16, 18) if n % k == 0)


def _scan(lo, hi, body, init, reverse=False):
    n = hi - lo
    if n <= 0:
        return init

    def wrapped(i, carry):
        return body(hi - 1 - i if reverse else lo + i, carry)

    return lax.fori_loop(0, n, wrapped, init)


def _head_norm_gate(o, nw_col, z):
    mu = jnp.mean(o, axis=0, keepdims=True)
    oc = o - mu
    var = jnp.mean(oc * oc, axis=0, keepdims=True)
    return oc * lax.rsqrt(var + EPS) * nw_col * _silu(z)


def _ret_kernel(lg_ref, qt_ref, k_ref, vt_ref, zt_ref, nw_ref, y_ref, kv, st, pt_scr, cr_scr,
                *, n_ctx_chunks, n_chunks, out_ctx, heads_per_step):
    for hh in range(heads_per_step):
        _ret_head(lg_ref, qt_ref, k_ref, vt_ref, zt_ref, nw_ref, y_ref, kv, st, pt_scr, cr_scr,
                  pl.program_id(1) * heads_per_step + hh, hh * D_HEAD, n_ctx_chunks, n_chunks, out_ctx)


def _ret_head(lg_ref, qt_ref, k_ref, vt_ref, zt_ref, nw_ref, y_ref, kv, st, pt_scr, cr_scr,
              h, r0, n_ctx_chunks, n_chunks, out_ctx):
    c_ = CHUNK
    hs = slice(r0, r0 + D_HEAD)
    lg_f = lg_ref[0, h]
    lg_b = lg_ref[1, h]
    lane = lax.broadcasted_iota(jnp.int32, (1, c_), 1).astype(F32)
    zeta_f = jnp.exp(lg_f * (c_ - 1.0 - lane)).astype(BF16)
    zeta_b = jnp.exp(lg_b * lane).astype(BF16)
    xi_f = jnp.exp(lg_f * (lane + 1.0))
    xi_b = jnp.exp(lg_b * (c_ - lane))
    dec_f = jnp.exp(lg_f * jnp.full((1, c_), float(c_), F32))
    dec_b = jnp.exp(lg_b * jnp.full((1, c_), float(c_), F32))
    ki = lax.broadcasted_iota(jnp.int32, (c_, c_), 0)
    qi = lax.broadcasted_iota(jnp.int32, (c_, c_), 1)
    diff = (qi - ki).astype(F32)
    dsum = (jnp.where(diff >= 0, jnp.exp(lg_f * jnp.maximum(diff, 0.0)), 0.0)
            + jnp.where(diff <= 0, jnp.exp(lg_b * jnp.maximum(-diff, 0.0)), 0.0))

    def outer_products(c):
        t0 = _chunk_start(c)
        vt = vt_ref[0, hs, pl.ds(t0, c_)]
        lhs = jnp.concatenate([vt * zeta_f, vt * zeta_b], axis=0)
        kv[c] = jnp.dot(lhs, k_ref[0, pl.ds(t0, c_), hs], preferred_element_type=F32)

    _loop(0, n_chunks, outer_products, unroll=_unroll(n_chunks))

    def scan_f(c, s):
        st[c, 0:c_, :] = s.astype(BF16)
        return dec_f * s + kv[c, 0:c_, :]

    def scan_b(c, s):
        st[c, c_:2 * c_, :] = s.astype(BF16)
        return dec_b * s + kv[c, c_:2 * c_, :]

    zero = jnp.zeros((c_, c_), F32)
    _scan(0, n_chunks, scan_f, zero)
    sb = _scan(0, n_ctx_chunks, scan_b, zero, reverse=True)
    _scan(n_ctx_chunks, n_chunks, scan_b, sb, reverse=True)

    nw_col = nw_ref[hs, :]

    def scores(c):
        t0 = _chunk_start(c)
        qt = qt_ref[0, hs, pl.ds(t0, c_)]
        at = jnp.dot(k_ref[0, pl.ds(t0, c_), hs], qt, preferred_element_type=F32)
        crs = jnp.dot(st[c], qt, preferred_element_type=F32)
        pt_scr[c] = (at * dsum).astype(BF16)
        cr_scr[c] = crs[0:c_, :] * xi_f + crs[c_:2 * c_, :] * xi_b

    def outputs(c):
        t0 = _chunk_start(c)
        o = jnp.dot(vt_ref[0, hs, pl.ds(t0, c_)], pt_scr[c], preferred_element_type=F32) + cr_scr[c]
        z = zt_ref[0, hs, pl.ds(t0, c_)].astype(F32)
        y_ref[0, hs, pl.ds(t0, c_)] = _head_norm_gate(o, nw_col, z).astype(BF16)

    lo = 0 if out_ctx else n_ctx_chunks
    if not out_ctx:
        y_ref[0, hs, 0:n_ctx_chunks * c_] = jnp.zeros((D_HEAD, n_ctx_chunks * c_), BF16)
    _loop(lo, n_chunks, scores, unroll=_unroll(n_chunks - lo))
    _loop(lo, n_chunks, outputs, unroll=_unroll(n_chunks - lo))


def _ret_mix(lg, ut, uk, nw, n_ctx, out_ctx):
    b, _, t = ut.shape
    n_chunks = t // CHUNK
    hd = D_HEAD
    hps = HEADS_PER_STEP
    blk = N_HEADS // hps

    def fm(name):
        g = FM_GROUPS.index(name)
        return pl.BlockSpec((1, hps * hd, t), lambda i, h: (i, g * blk + h, 0))

    tm_rk = TM_GROUPS.index("rk")
    kern = functools.partial(_ret_kernel, n_ctx_chunks=n_ctx // CHUNK, n_chunks=n_chunks, out_ctx=out_ctx,
                             heads_per_step=hps)
    return pl.pallas_call(
        kern,
        grid=(b, blk),
        in_specs=[pl.BlockSpec(memory_space=pltpu.SMEM),
                  fm("rq"),
                  pl.BlockSpec((1, t, hps * hd), lambda i, h: (i, 0, tm_rk * blk + h)),
                  fm("rv"), fm("rz"),
                  pl.BlockSpec((hps * hd, 1), lambda i, h: (h, 0))],
        out_specs=pl.BlockSpec((1, hps * hd, t), lambda i, h: (i, h, 0)),
        out_shape=jax.ShapeDtypeStruct((b, N_HEADS * hd, t), BF16),
        scratch_shapes=[pltpu.VMEM((n_chunks, 2 * hd, hd), F32),
                        pltpu.VMEM((n_chunks, 2 * hd, hd), BF16),
                        pltpu.VMEM((n_chunks, hd, hd), BF16),
                        pltpu.VMEM((n_chunks, hd, hd), F32)],
        compiler_params=_params("parallel", "arbitrary"),
        name="ret_mix",
    )(lg, ut, uk, ut, ut, nw)


def _seg_scan(v, pos, op, fill, reverse):
    t = v.shape[1]
    step = 1
    while step < CHUNK:
        if reverse:
            v = op(v, jnp.where(pos < CHUNK - step, pltpu.roll(v, t - step, 1), fill))
        else:
            v = op(v, jnp.where(pos >= step, pltpu.roll(v, step, 1), fill))
        step *= 2
    return v


def _gate_kernel(g_ref, o_ref, tmp, *, n_ctx_chunks, n_chunks):
    c_ = CHUNK
    nh = N_HEADS
    g = g_ref[0]
    pos = lax.broadcasted_iota(jnp.int32, (nh, g.shape[1]), 1) & (c_ - 1)
    for d in range(2):
        rev = d == 1
        ig = g[2 * d * nh:(2 * d + 1) * nh]
        lf = jax.nn.log_sigmoid(g[(2 * d + 1) * nh:(2 * d + 2) * nh])
        bcum = _seg_scan(lf, pos, jnp.add, 0.0, rev)
        tot = bcum + _seg_scan(lf, pos, jnp.add, 0.0, not rev) - lf
        r = ig - bcum
        cm = _seg_scan(r, pos, jnp.maximum, NEG, rev)
        rmax = jnp.maximum(cm, _seg_scan(r, pos, jnp.maximum, NEG, not rev))
        o_ref[0, 2 * GQ_ROWK + d] = r
        tmp[2 * GT_CM + d] = cm
        tmp[2 * GT_TOT + d] = tot
        tmp[2 * GT_RMAX + d] = rmax
        tmp[2 * GT_BCUM + d] = bcum

    def make_step(d):
        def step(c, m_prev):
            sl = pl.ds(_chunk_start(c), c_)
            gt = tmp[2 * GT_TOT + d, :, sl]
            m_new = jnp.maximum(gt + m_prev, gt + tmp[2 * GT_RMAX + d, :, sl])
            mx = jnp.maximum(tmp[2 * GT_CM + d, :, sl], m_prev)
            o_ref[0, 2 * GQ_COLQ + d, :, sl] = -mx
            o_ref[0, 2 * GQ_SC + d, :, sl] = jnp.exp(m_prev - mx)
            o_ref[0, 2 * GQ_EMM + d, :, sl] = jnp.exp(-(tmp[2 * GT_BCUM + d, :, sl] + mx))
            o_ref[0, 2 * GQ_EW + d, :, sl] = jnp.exp(gt + o_ref[0, 2 * GQ_ROWK + d, :, sl] - m_new)
            o_ref[0, 2 * GQ_DEC + d, :, sl] = jnp.exp(gt + m_prev - m_new)
            return m_new
        return step

    m0 = jnp.zeros((nh, c_), F32)
    _scan(0, n_chunks, make_step(0), m0)
    mb = _scan(0, n_ctx_chunks, make_step(1), m0, reverse=True)
    _scan(n_ctx_chunks, n_chunks, make_step(1), mb, reverse=True)


def _gate_rows(gt, n_ctx):
    b, n_rows, t = gt.shape
    kern = functools.partial(_gate_kernel, n_ctx_chunks=n_ctx // CHUNK, n_chunks=t // CHUNK)
    rows = pl.pallas_call(
        kern,
        grid=(b,),
        in_specs=[pl.BlockSpec((1, n_rows, t), lambda i: (i, 0, 0))],
        out_specs=pl.BlockSpec((1, N_GQ, N_HEADS, t), lambda i: (i, 0, 0, 0)),
        out_shape=jax.ShapeDtypeStruct((b, N_GQ, N_HEADS, t), F32),
        scratch_shapes=[pltpu.VMEM((8, N_HEADS, t), F32)],
        compiler_params=_params("parallel"),
        name="ml_gates",
    )(gt)
    return rows.transpose(0, 2, 1, 3)


def _ml_kernel(q_ref, k_ref, vt_ref, ot_ref, zt_ref, gs_ref, nw_ref, y_ref, kv, st, pt_scr, cr_scr,
               *, n_ctx_chunks, n_chunks, out_ctx, heads_per_step):
    for hh in range(heads_per_step):
        _ml_head(q_ref, k_ref, vt_ref, ot_ref, zt_ref, gs_ref, nw_ref, y_ref, kv, st, pt_scr, cr_scr,
                 hh, n_ctx_chunks, n_chunks, out_ctx)


def _ml_head(q_ref, k_ref, vt_ref, ot_ref, zt_ref, gs_ref, nw_ref, y_ref, kv, st, pt_scr, cr_scr,
             hh, n_ctx_chunks, n_chunks, out_ctx):
    c_ = CHUNK
    hs = slice(hh * D_HEAD, (hh + 1) * D_HEAD)

    def grow(q, d, sl):
        return gs_ref[0, hh, 2 * q + d:2 * q + d + 1, sl]

    ones = jnp.ones((ONES_ROWS, c_), BF16)

    def outer_products(c):
        sl = pl.ds(_chunk_start(c), c_)
        vt = vt_ref[0, hs, sl]
        ew_f = grow(GQ_EW, 0, sl).astype(BF16)
        ew_b = grow(GQ_EW, 1, sl).astype(BF16)
        lhs = jnp.concatenate([vt * ew_f, ones * ew_f, vt * ew_b, ones * ew_b], axis=0)
        kv[c] = jnp.dot(lhs, k_ref[0, sl, hs], preferred_element_type=F32)

    _loop(0, n_chunks, outer_products, unroll=_unroll(n_chunks))

    def make_scan(d):
        def step(c, s):
            sl = pl.ds(_chunk_start(c), c_)
            st[c, d * AUG:(d + 1) * AUG, :] = s.astype(BF16)
            return grow(GQ_DEC, d, sl) * s + kv[c, d * AUG:(d + 1) * AUG, :]
        return step

    zero = jnp.zeros((AUG, c_), F32)
    _scan(0, n_chunks, make_scan(0), zero)
    sb = _scan(0, n_ctx_chunks, make_scan(1), zero, reverse=True)
    _scan(n_ctx_chunks, n_chunks, make_scan(1), sb, reverse=True)

    nw_col = nw_ref[hs, :]
    ki = lax.broadcasted_iota(jnp.int32, (c_, c_), 0)
    qi = lax.broadcasted_iota(jnp.int32, (c_, c_), 1)

    def scores(c):
        sl = pl.ds(_chunk_start(c), c_)
        qc = q_ref[0, sl, hs]
        at = lax.dot_general(k_ref[0, sl, hs], qc, NT_DIMS, preferred_element_type=F32)
        crs = lax.dot_general(st[c], qc, NT_DIMS, preferred_element_type=F32)
        rowk_f = jnp.broadcast_to(grow(GQ_ROWK, 0, sl), (c_, c_)).T
        rowk_b = jnp.broadcast_to(grow(GQ_ROWK, 1, sl), (c_, c_)).T
        s_f = at * jnp.exp(jnp.where(ki <= qi, rowk_f + grow(GQ_COLQ, 0, sl), NEG))
        s_b = at * jnp.exp(jnp.where(ki >= qi, rowk_b + grow(GQ_COLQ, 1, sl), NEG))
        sc_f = grow(GQ_SC, 0, sl)
        sc_b = grow(GQ_SC, 1, sl)
        den_f = jnp.sum(s_f, axis=0, keepdims=True) + crs[D_HEAD:D_HEAD + 1, :] * sc_f
        den_b = jnp.sum(s_b, axis=0, keepdims=True) + crs[AUG + D_HEAD:AUG + D_HEAD + 1, :] * sc_b
        r_f = 1.0 / jnp.maximum(jnp.abs(den_f), grow(GQ_EMM, 0, sl))
        r_b = 1.0 / jnp.maximum(jnp.abs(den_b), grow(GQ_EMM, 1, sl))
        pt_scr[c] = (s_f * r_f + s_b * r_b).astype(BF16)
        cr_scr[c] = crs[0:D_HEAD, :] * (sc_f * r_f) + crs[AUG:AUG + D_HEAD, :] * (sc_b * r_b)

    def outputs(c):
        sl = pl.ds(_chunk_start(c), c_)
        o = jnp.dot(vt_ref[0, hs, sl], pt_scr[c], preferred_element_type=F32) + cr_scr[c]
        o = jax.nn.sigmoid(ot_ref[0, hs, sl].astype(F32)) * o
        z = zt_ref[0, hs, sl].astype(F32)
        y_ref[0, hs, sl] = _head_norm_gate(o, nw_col, z).astype(BF16)

    lo = 0 if out_ctx else n_ctx_chunks
    if not out_ctx:
        y_ref[0, hs, 0:n_ctx_chunks * c_] = jnp.zeros((D_HEAD, n_ctx_chunks * c_), BF16)
    _loop(lo, n_chunks, scores, unroll=_unroll(n_chunks - lo))
    _loop(lo, n_chunks, outputs, unroll=_unroll(n_chunks - lo))


def _ml_mix(ut, uk, gs, nw, n_ctx, out_ctx):
    b, _, t = ut.shape
    n_chunks = t // CHUNK
    hd = D_HEAD
    hps = HEADS_PER_STEP
    blk = N_HEADS // hps

    def fm(name):
        g = FM_GROUPS.index(name)
        return pl.BlockSpec((1, hps * hd, t), lambda i, h: (i, g * blk + h, 0))

    def tm(name):
        g = TM_GROUPS.index(name)
        return pl.BlockSpec((1, t, hps * hd), lambda i, h: (i, 0, g * blk + h))

    kern = functools.partial(_ml_kernel, n_ctx_chunks=n_ctx // CHUNK, n_chunks=n_chunks, out_ctx=out_ctx,
                             heads_per_step=hps)
    return pl.pallas_call(
        kern,
        grid=(b, blk),
        in_specs=[tm("mq"), tm("mk"), fm("mv"), fm("mo"), fm("mz"),
                  pl.BlockSpec((1, hps, N_GQ, t), lambda i, h: (i, h, 0, 0)),
                  pl.BlockSpec((hps * hd, 1), lambda i, h: (h, 0))],
        out_specs=pl.BlockSpec((1, hps * hd, t), lambda i, h: (i, h, 0)),
        out_shape=jax.ShapeDtypeStruct((b, N_HEADS * hd, t), BF16),
        scratch_shapes=[pltpu.VMEM((n_chunks, 2 * AUG, hd), F32),
                        pltpu.VMEM((n_chunks, 2 * AUG, hd), BF16),
                        pltpu.VMEM((n_chunks, hd, hd), BF16),
                        pltpu.VMEM((n_chunks, hd, hd), F32)],
        compiler_params=_params("parallel", "arbitrary"),
        name="ml_mix",
    )(uk, uk, ut, ut, ut, gs, nw)


def _out_kernel(yr_ref, ym_ref, gr_ref, gm_ref, wro_ref, wmo_ref, wout_ref, x_ref, gate_ref, fnw_ref,
                o_ref, *, final):
    br = jnp.dot(wro_ref[...], yr_ref[0], preferred_element_type=F32)
    bm = jnp.dot(wmo_ref[...], ym_ref[0], preferred_element_type=F32)
    y = (jax.nn.sigmoid(gr_ref[0].astype(F32)) * br + jax.nn.sigmoid(gm_ref[0].astype(F32)) * bm).astype(BF16)
    o = lax.dot_general(y, wout_ref[...], TN_DIMS, preferred_element_type=F32)
    xn = x_ref[0] + gate_ref[0] * o
    if final:
        ms = jnp.mean(xn * xn, axis=-1, keepdims=True)
        xn = xn * lax.rsqrt(ms + EPS) * fnw_ref[...]
    o_ref[0] = xn


def _out(yr, ym, ut, wro_t, wmo_t, wout, xs, gate, fnw, tok0, final):
    b, n, d = xs.shape
    tm = 2 * ROW_BLOCK if n % (2 * ROW_BLOCK) == 0 else ROW_BLOCK
    assert n % tm == 0
    gr_row = FM_GROUPS.index("gr") * N_HEADS * D_HEAD
    gm_row = FM_GROUPS.index("gm") * N_HEADS * D_HEAD

    def tok(row0):
        return pl.BlockSpec((pl.Element(1), pl.Element(d), pl.Element(tm)), lambda i, j: (i, pl.multiple_of(j * 0 + row0, 128), pl.multiple_of(tok0 + j * tm, 128)))

    kern = functools.partial(_out_kernel, final=final)
    return pl.pallas_call(
        kern,
        grid=(b, n // tm),
        in_specs=[tok(0), tok(0), tok(gr_row), tok(gm_row),
                  pl.BlockSpec((d, d), lambda i, j: (0, 0)),
                  pl.BlockSpec((d, d), lambda i, j: (0, 0)),
                  pl.BlockSpec((d, d), lambda i, j: (0, 0)),
                  pl.BlockSpec((1, tm, d), lambda i, j: (i, j, 0)),
                  pl.BlockSpec((1, 1, d), lambda i, j: (i, 0, 0)),
                  pl.BlockSpec((1, d), lambda i, j: (0, 0))],
        out_specs=pl.BlockSpec((1, tm, d), lambda i, j: (i, j, 0)),
        out_shape=jax.ShapeDtypeStruct((b, n, d), F32),
        compiler_params=_params("parallel", "parallel"),
        name="out_final" if final else "out",
    )(yr, ym, ut, ut, wro_t, wmo_t, wout, xs, gate, fnw)


def _rope_tables(n_lat):
    rows_n = n_lat // GRID_W
    rows = jnp.repeat(jnp.arange(rows_n, dtype=F32), GRID_W)
    cols = jnp.tile(jnp.arange(GRID_W, dtype=F32), rows_n)
    nf = D_HEAD // 4
    freqs = ROPE_BASE ** (-jnp.arange(nf, dtype=F32) / nf)
    ang = jnp.concatenate([rows[:, None] * freqs, cols[:, None] * freqs], axis=-1)
    cos, sin = jnp.cos(ang), jnp.sin(ang)
    cos_k = jnp.concatenate([cos, cos], axis=-1)
    sin_k = jnp.concatenate([-sin, sin], axis=-1)
    return cos_k.T, sin_k.T, cos_k, sin_k


def _split_weights(w_in, b_in, conv_w, conv_b):
    hd = N_HEADS * D_HEAD
    d = w_in.shape[0]
    off = {name: i * hd for i, name in enumerate(("rq", "rk", "rv", "rz", "mq", "mk", "mv", "mo", "mz"))}
    g_off = 9 * hd
    n_g = 4 * N_HEADS
    off["gr"] = g_off + n_g
    off["gm"] = g_off + n_g + d
    widths = dict.fromkeys(off, hd)
    widths["gr"] = widths["gm"] = d

    def cols(a, name):
        return a[..., off[name]:off[name] + widths[name]]

    wt = jnp.concatenate([cols(w_in, n) for n in FM_GROUPS], axis=-1).T.astype(BF16)
    bt = jnp.concatenate([cols(b_in, n) for n in FM_GROUPS], axis=-1)[:, None]
    wk = jnp.concatenate([cols(w_in, n) for n in TM_GROUPS], axis=-1).astype(BF16)
    bk = jnp.concatenate([cols(b_in, n) for n in TM_GROUPS], axis=-1)[None, :]
    wg = w_in[:, g_off:g_off + n_g].T.astype(BF16)
    bg = b_in[g_off:g_off + n_g][:, None]
    conv_cols = {"mq": slice(0, hd), "mk": slice(hd, 2 * hd)}
    cw = jnp.concatenate([conv_w[:, conv_cols[n]] if n in conv_cols else jnp.zeros((CONV_W, hd), F32)
                          for n in TM_GROUPS], axis=-1)
    cb = jnp.concatenate([conv_b[conv_cols[n]] if n in conv_cols else jnp.zeros((hd,), F32)
                          for n in TM_GROUPS])[None, :]
    ps = jnp.concatenate([jnp.full((hd,), D_HEAD ** -0.5 if n == "mq" else 1.0, F32) for n in TM_GROUPS])[None, :]
    return wt, bt, wg, bg, wk, bk, cw, cb, ps


def kernel(x, c, ctx, c_ctx, norm_w, w_ada, b_ada, w_in, b_in, conv_w, conv_b, ret_log_gamma, ret_norm_w,
           ml_norm_w, w_ret_o, w_ml_o, w_out, final_norm_w):
    b, n_lat, d = x.shape
    n_ctx = ctx.shape[1]
    depth = w_in.shape[0]
    hd = N_HEADS * D_HEAD
    assert d == hd and n_lat % ROW_BLOCK == 0 and n_ctx % ROW_BLOCK == 0 and n_lat % GRID_W == 0
    assert TM_GROUPS[0] == "rk" and FM_GROUPS[0] == "rq"

    rows = -(-(b + 1) // SUBLANES) * SUBLANES
    cc = jnp.zeros((rows, d), F32).at[:b].set(c).at[b].set(c_ctx)
    mods = _adaln(cc, w_ada, b_ada)
    tables = _rope_tables(n_lat)

    for l in range(depth):
        final = l == depth - 1
        mx = mods[l, :b].reshape(b, 3, d)
        mc = mods[l, b].reshape(3, d)
        w = _split_weights(w_in[l], b_in[l], conv_w[l], conv_b[l])
        ut, gt, uk = _proj(ctx, x, norm_w[l][None, :], mx, mc, w, tables, not final)
        yr = _ret_mix(ret_log_gamma[l], ut, uk, ret_norm_w[l][:, None], n_ctx, not final)
        gs = _gate_rows(gt, n_ctx)
        ym = _ml_mix(ut, uk, gs, ml_norm_w[l][:, None], n_ctx, not final)
        wo = (w_ret_o[l].T.astype(BF16), w_ml_o[l].T.astype(BF16), w_out[l].astype(BF16))
        fnw = final_norm_w[None, :]
        if not final:
            ctx = _out(yr, ym, ut, *wo, ctx, jnp.broadcast_to(mc[2], (b, 1, d)), fnw, 0, False)
        x = _out(yr, ym, ut, *wo, x, mx[:, 2:3], fnw, n_ctx, final)
    return x
```
